```python
import math
import jax, jax.numpy as jnp
from jax import lax
import numpy as np

D_MODEL = 1024
BATCH = 2
SEQ = 16384
DEPTH = 4

EPS = 1e-6
A_WIDTH = 256
A_GROUPS = 4
A_GROUP_DIM = A_WIDTH // A_GROUPS
A_CHUNK = 128
MLA_HEADS = 8
Q_LORA = 384
KV_LORA = 256
QK_NOPE = 64
QK_ROPE = 32
V_DIM = 64
QK_DIM = QK_NOPE + QK_ROPE
ROPE_THETA = 10000.0
Q_BLOCK = 128
POOL_WINDOWS = (2, 4, 8, 16)
C_WIDTH = 256
C_GROUP = C_WIDTH // len(POOL_WINDOWS)
N_BRANCH = 3
SPLITS = (A_WIDTH, 2 * A_WIDTH, 2 * A_WIDTH + Q_LORA, 2 * A_WIDTH + Q_LORA + KV_LORA,
          2 * A_WIDTH + Q_LORA + KV_LORA + QK_ROPE,
          2 * A_WIDTH + Q_LORA + KV_LORA + QK_ROPE + C_WIDTH)
IN_COLS = SPLITS[-1] + N_BRANCH * D_MODEL
PEER_HEADS = 8
N_KEYS = 128
N_EXPERTS = N_KEYS * N_KEYS
PEER_DKEY = 256
PEER_HALF = PEER_DKEY // 2
PEER_TOPK = 16
PEER_BLOCK = 128

kernel_name = "hybrid_gated_mixers_peer_trunk"


def _rmsnorm(x, gain=None):
    xf = x.astype(jnp.float32)
    y = xf * lax.rsqrt(jnp.mean(xf * xf, axis=-1, keepdims=True) + EPS)
    if gain is not None:
        y = y * gain.astype(jnp.float32)
    return y.astype(x.dtype)


def _layernorm(x):
    xf = x.astype(jnp.float32)
    mu = jnp.mean(xf, axis=-1, keepdims=True)
    d = xf - mu
    return (d * lax.rsqrt(jnp.mean(d * d, axis=-1, keepdims=True) + EPS)).astype(x.dtype)


def _modulate(h, shift, scale):
    return h * (1 + scale[:, None, :]) + shift[:, None, :]


def _rope(x, cos, sin):
    half = x.shape[-1] // 2
    x1, x2 = x[..., :half], x[..., half:]
    cos = cos.astype(x.dtype)
    sin = sin.astype(x.dtype)
    return jnp.concatenate([x1 * cos - x2 * sin, x2 * cos + x1 * sin], axis=-1)


def _spatial_gating(u, v, w_s, b_s):
    B, S, _ = u.shape
    vn = _layernorm(v).reshape(B, S // A_CHUNK, A_CHUNK, A_GROUPS, A_GROUP_DIM)
    mixed = jnp.einsum('gts,bnsgc->bntgc', jnp.tril(w_s).astype(vn.dtype), vn)
    mixed = mixed + b_s.T.astype(vn.dtype)[:, :, None]
    return u * mixed.reshape(B, S, A_WIDTH)


def _mla(cq, ckv, kr, q_norm, w_uq, kv_norm, w_ukv, cos, sin):
    B, S, _ = cq.shape
    q = (_rmsnorm(cq, q_norm) @ w_uq).reshape(B, S, MLA_HEADS, QK_DIM)
    q = jnp.concatenate([q[..., :QK_NOPE], _rope(q[..., QK_NOPE:], cos, sin)], axis=-1)
    kv = (_rmsnorm(ckv, kv_norm) @ w_ukv).reshape(B, S, MLA_HEADS, QK_NOPE + V_DIM)
    k_rope = _rope(kr[:, :, None, :], cos, sin)
    k = jnp.concatenate([kv[..., :QK_NOPE],
                         jnp.broadcast_to(k_rope, (B, S, MLA_HEADS, QK_ROPE))], axis=-1)
    v = kv[..., QK_NOPE:]
    q = q.transpose(0, 2, 1, 3) * (1.0 / math.sqrt(QK_DIM))
    k = k.transpose(0, 2, 1, 3)
    v = v.transpose(0, 2, 1, 3)
    nq = S // Q_BLOCK
    qb = q.reshape(B, MLA_HEADS, nq, Q_BLOCK, QK_DIM).transpose(2, 0, 1, 3, 4)
    key_pos = jnp.arange(S)

    def block(args):
        qi, bi = args
        s = jnp.einsum('bhqd,bhkd->bhqk', qi, k).astype(jnp.float32)
        q_pos = bi * Q_BLOCK + jnp.arange(Q_BLOCK)
        s = jnp.where(key_pos[None, :] <= q_pos[:, None], s, -jnp.inf)
        p = jax.nn.softmax(s, axis=-1)
        return jnp.einsum('bhqk,bhkd->bhqd', p.astype(v.dtype), v)

    o = lax.map(block, (qb, jnp.arange(nq)))
    o = o.transpose(1, 0, 3, 2, 4).reshape(B, S, MLA_HEADS * V_DIM)
    return o


def _multiscale_pool(z, w_pool, pool_scale):
    B, S, _ = z.shape
    zf = z.astype(jnp.float32)
    t = jnp.arange(S)
    outs = []
    for gi, w in enumerate(POOL_WINDOWS):
        zg = zf[..., gi * C_GROUP:(gi + 1) * C_GROUP]
        cs = jnp.cumsum(jnp.pad(zg, ((0, 0), (w, 0), (0, 0))), axis=1)
        win = cs[:, w:] - cs[:, :S]
        cnt = jnp.minimum(t + 1, w).astype(jnp.float32)[None, :, None]
        outs.append((win / cnt - zg) @ w_pool[gi].astype(jnp.float32))
    y = jnp.concatenate(outs, axis=-1) * pool_scale.astype(jnp.float32)
    return y.astype(z.dtype)


def _peer(h, w_pq, sub_keys, u_tab, v_tab):
    B, S, D = h.shape
    T = B * S
    xb = h.reshape(T // PEER_BLOCK, PEER_BLOCK, D)

    def block(xt):
        q = (xt @ w_pq).reshape(PEER_BLOCK, PEER_HEADS, 2, PEER_HALF)
        s = jnp.einsum('thpd,hpkd->thpk', q, sub_keys)
        v1, i1 = lax.top_k(s[:, :, 0], PEER_TOPK)
        v2, i2 = lax.top_k(s[:, :, 1], PEER_TOPK)
        cand = (v1[..., :, None] + v2[..., None, :]).reshape(PEER_BLOCK, PEER_HEADS, -1)
        cidx = (i1[..., :, None] * N_KEYS + i2[..., None, :]).reshape(PEER_BLOCK, PEER_HEADS, -1)
        top, pos = lax.top_k(cand, PEER_TOPK)
        eidx = jnp.take_along_axis(cidx, pos, axis=-1)
        g = jax.nn.softmax(top.astype(jnp.float32), axis=-1).astype(xt.dtype)
        ue = u_tab[eidx]
        a = jax.nn.gelu(jnp.einsum('thkd,td->thk', ue, xt))
        ve = v_tab[eidx]
        return jnp.einsum('thk,thkd->td', g * a, ve)

    return lax.map(block, xb).reshape(B, S, D)


def setup_inputs(seed: int = 0) -> dict:
    key = jax.random.key(seed)
    ks = jax.random.split(key, 24)
    f32 = jnp.float32
    nrm = lambda k, shape, s: jax.random.normal(k, shape, f32) * s
    L, D = DEPTH, D_MODEL
    x = nrm(ks[0], (BATCH, SEQ, D), 1.0)
    c = nrm(ks[1], (BATCH, D), 1.0)
    offs = jax.random.randint(ks[2], (BATCH, 1), 0, 4096, dtype=jnp.int32)
    positions = (offs + jnp.arange(SEQ, dtype=jnp.int32)[None, :]).astype(jnp.int32)
    return {
        "x": x,
        "c": c,
        "positions": positions,
        "w_mod": nrm(ks[3], (L, D, 6 * D), 0.5 * D ** -0.5),
        "b_mod": nrm(ks[4], (L, 6 * D), 0.02),
        "w_in": nrm(ks[5], (L, D, IN_COLS), D ** -0.5),
        "w_s": nrm(ks[6], (L, A_GROUPS, A_CHUNK, A_CHUNK), A_CHUNK ** -0.5),
        "b_s": 1.0 + nrm(ks[7], (L, A_GROUPS, A_CHUNK), 0.02),
        "q_norm": 1.0 + nrm(ks[8], (L, Q_LORA), 0.02),
        "w_uq": nrm(ks[9], (L, Q_LORA, MLA_HEADS * QK_DIM), Q_LORA ** -0.5),
        "kv_norm": 1.0 + nrm(ks[10], (L, KV_LORA), 0.02),
        "w_ukv": nrm(ks[11], (L, KV_LORA, MLA_HEADS * (QK_NOPE + V_DIM)), KV_LORA ** -0.5),
        "w_pool": nrm(ks[12], (L, len(POOL_WINDOWS), C_GROUP, C_GROUP), C_GROUP ** -0.5),
        "pool_scale": 1.0 + nrm(ks[13], (L, C_WIDTH), 0.02),
        "w_a": nrm(ks[14], (L, A_WIDTH, D), A_WIDTH ** -0.5),
        "w_b": nrm(ks[15], (L, MLA_HEADS * V_DIM, D), (MLA_HEADS * V_DIM) ** -0.5),
        "w_c": nrm(ks[16], (L, C_WIDTH, D), C_WIDTH ** -0.5),
        "w_o": nrm(ks[17], (L, D, D), D ** -0.5),
        "w_pq": nrm(ks[18], (L, D, PEER_HEADS * PEER_DKEY), D ** -0.5),
        "sub_keys": nrm(ks[19], (L, PEER_HEADS, 2, N_KEYS, PEER_HALF), PEER_HALF ** -0.5),
        "u_tab": nrm(ks[20], (L, N_EXPERTS, D), D ** -0.5),
        "v_tab": nrm(ks[21], (L, N_EXPERTS, D), 0.5),
        "final_norm": 1.0 + nrm(ks[22], (D,), 0.02),
    }


def reference(x, c, positions, w_mod, b_mod, w_in, w_s, b_s, q_norm, w_uq, kv_norm, w_ukv,
              w_pool, pool_scale, w_a, w_b, w_c, w_o, w_pq, sub_keys, u_tab, v_tab, final_norm):
    B, S, D = x.shape
    inv_freq = 1.0 / (ROPE_THETA ** (jnp.arange(0, QK_ROPE, 2, dtype=jnp.float32) / QK_ROPE))
    ang = positions.astype(jnp.float32)[..., None] * inv_freq
    cos = jnp.cos(ang)[:, :, None, :]
    sin = jnp.sin(ang)[:, :, None, :]
    c_act = jax.nn.silu(c)
    for l in range(DEPTH):
        mod = c_act @ w_mod[l] + b_mod[l]
        sh1, sc1, g1, sh2, sc2, g2 = jnp.split(mod, 6, axis=-1)
        h = _modulate(_rmsnorm(x), sh1, sc1)
        z = h @ w_in[l]
        zu, zv, cq, ckv, kr, zp, zg = jnp.split(z, SPLITS, axis=-1)
        ya = _spatial_gating(jax.nn.gelu(zu), jax.nn.gelu(zv), w_s[l], b_s[l])
        yb = _mla(cq, ckv, kr, q_norm[l], w_uq[l], kv_norm[l], w_ukv[l], cos, sin)
        yc = _multiscale_pool(zp, w_pool[l], pool_scale[l])
        gates = jax.nn.sigmoid(zg).reshape(B, S, N_BRANCH, D)
        merged = (gates[:, :, 0] * (ya @ w_a[l]) + gates[:, :, 1] * (yb @ w_b[l])
                  + gates[:, :, 2] * (yc @ w_c[l]))
        x = x + g1[:, None, :] * (merged @ w_o[l])
        h2 = _modulate(_rmsnorm(x), sh2, sc2)
        x = x + g2[:, None, :] * _peer(h2, w_pq[l], sub_keys[l], u_tab[l], v_tab[l])
    return _rmsnorm(x, final_norm)
```

```python
import functools
import math

import numpy as np
import jax
import jax.numpy as jnp
from jax import lax
from jax.experimental import pallas as pl
from jax.experimental.pallas import tpu as pltpu

F32 = jnp.float32
BF16 = jnp.bfloat16

EPS = 1e-6
D_MODEL = 1024
A_WIDTH = 256
A_GROUPS = 4
A_GROUP_DIM = A_WIDTH // A_GROUPS
A_CHUNK = 128
MLA_HEADS = 8
Q_LORA = 384
KV_LORA = 256
QK_NOPE = 64
QK_ROPE = 32
V_DIM = 64
QK_DIM = QK_NOPE + QK_ROPE
ROPE_THETA = 10000.0
POOL_WINDOWS = (2, 4, 8, 16)
C_WIDTH = 256
C_GROUP = C_WIDTH // len(POOL_WINDOWS)
PEER_HEADS = 8
N_KEYS = 128
N_EXPERTS = N_KEYS * N_KEYS
PEER_HALF = 128
PEER_TOPK = 16
N_PAIRS = PEER_HEADS * PEER_TOPK

LANES = 128
SUBLANES = 8
HEAD_PAD = LANES
HALO = 16
HALF_EXPERTS = N_EXPERTS // 2
NEG = -1e30

VMEM_LIMIT = 48 * 1024 * 1024
TABLE_VMEM_LIMIT = 56 * 1024 * 1024

_C_ZU, _C_ZV, _C_CQ, _C_CKV, _C_ZP, _C_KR, _C_END = 0, 256, 512, 896, 1152, 1408, 1536


def _rms(x):
    return x * lax.rsqrt(jnp.mean(x * x, axis=-1, keepdims=True) + EPS)


def _cparams(sem, limit=VMEM_LIMIT):
    return pltpu.CompilerParams(dimension_semantics=sem, vmem_limit_bytes=limit)


def _mod_kernel(c_ref, w_ref, b_ref, o_ref):
    c = c_ref[...]
    ca = c * jax.nn.sigmoid(c)
    o_ref[0] = jnp.dot(ca, w_ref[0], preferred_element_type=F32) + b_ref[0]


def _modulation(c_pad, w_mod, b_mod):
    L, D, six_d = w_mod.shape
    nb = six_d // D
    rows = c_pad.shape[0]
    return pl.pallas_call(
        _mod_kernel,
        grid=(L, nb),
        in_specs=[
            pl.BlockSpec((rows, D), lambda l, j: (0, 0)),
            pl.BlockSpec((1, D, D), lambda l, j: (l, 0, j)),
            pl.BlockSpec((1, 1, D), lambda l, j: (l, 0, j)),
        ],
        out_specs=pl.BlockSpec((1, rows, D), lambda l, j: (l, 0, j)),
        out_shape=jax.ShapeDtypeStruct((L, rows, six_d), F32),
        compiler_params=_cparams(("arbitrary", "arbitrary")),
        name="modulation",
    )(c_pad, w_mod, b_mod.reshape(L, 1, six_d))


def _rope_kernel(ang_ref, cos_ref, sin_ref):
    ang = ang_ref[...]
    lane = lax.broadcasted_iota(jnp.int32, ang.shape, 1)
    first_half = lane < QK_NOPE + QK_ROPE // 2
    cos_ref[...] = jnp.cos(ang)
    s = jnp.sin(ang)
    sin_ref[...] = jnp.where(first_half, -s, s)


def _rope_tables(ang, tb):
    T = ang.shape[0]
    spec = pl.BlockSpec((tb, LANES), lambda i: (i, 0))
    return pl.pallas_call(
        _rope_kernel,
        grid=(T // tb,),
        in_specs=[spec],
        out_specs=[spec, spec],
        out_shape=[jax.ShapeDtypeStruct((T, LANES), F32)] * 2,
        compiler_params=_cparams(("arbitrary",)),
        name="rope_tables",
    )(ang)


def _rope_apply(xh, cos, sin):
    lane = lax.broadcasted_iota(jnp.int32, xh.shape, 1)
    half = QK_ROPE // 2
    partner = jnp.where(lane < QK_NOPE + half,
                        pltpu.roll(xh, LANES - half, axis=1),
                        pltpu.roll(xh, half, axis=1))
    return xh * cos + partner * sin


def _inproj_kernel(x_ref, xh_ref, mod_ref, w1_ref, ws_ref, bs_ref, qn_ref, wuq_ref, kvn_ref,
                   wk_ref, wv_ref, cos_ref, sin_ref, wpool_ref, pscale_ref,
                   ya_ref, yc_ref, q_ref, k_ref, v_ref, *, seq, tb):
    i = pl.program_id(0)
    m = mod_ref[0]
    sh1, sc1 = m[0:1], m[1:2]
    h = (_rms(x_ref[...]) * (1.0 + sc1) + sh1).astype(BF16)
    z = jnp.dot(h, w1_ref[...], preferred_element_type=F32)

    gu = jax.nn.gelu(z[:, _C_ZU:_C_ZV])
    gv = jax.nn.gelu(z[:, _C_ZV:_C_CQ])
    mu = jnp.mean(gv, axis=-1, keepdims=True)
    dv = gv - mu
    vn = (dv * lax.rsqrt(jnp.mean(dv * dv, axis=-1, keepdims=True) + EPS)).astype(BF16)
    r = lax.broadcasted_iota(jnp.int32, (A_CHUNK, A_CHUNK), 0)
    cc = lax.broadcasted_iota(jnp.int32, (A_CHUNK, A_CHUNK), 1)
    lane_a = lax.broadcasted_iota(jnp.int32, (A_CHUNK, A_WIDTH), 1)
    wtril = [jnp.where(r >= cc, ws_ref[g], 0.0).astype(BF16) for g in range(A_GROUPS)]
    for ci in range(tb // A_CHUNK):
        lo = ci * A_CHUNK
        vc = vn[lo:lo + A_CHUNK]
        mixed = bs_ref[...]
        for g in range(A_GROUPS):
            dg = jnp.dot(wtril[g], vc, preferred_element_type=F32)
            in_g = (lane_a >= g * A_GROUP_DIM) & (lane_a < (g + 1) * A_GROUP_DIM)
            mixed = mixed + jnp.where(in_g, dg, 0.0)
        ya_ref[lo:lo + A_CHUNK, :] = (gu[lo:lo + A_CHUNK] * mixed).astype(BF16)

    zp = z[:, _C_ZP:_C_KR]
    t0 = (i * tb) % seq
    hh = (_rms(xh_ref[...]) * (1.0 + sc1) + sh1).astype(BF16)
    zph = jnp.dot(hh, w1_ref[:, _C_ZP:_C_KR], preferred_element_type=F32)
    zph = zph * jnp.where(t0 == 0, 0.0, 1.0)
    zext = jnp.concatenate([zph, zp], axis=0)
    n_ext = HALO + tb
    s2 = zext + pltpu.roll(zext, 1, axis=0)
    s4 = s2 + pltpu.roll(s2, 2, axis=0)
    s8 = s4 + pltpu.roll(s4, 4, axis=0)
    s16 = s8 + pltpu.roll(s8, 8, axis=0)
    lane_c = lax.broadcasted_iota(jnp.int32, (n_ext, C_WIDTH), 1)
    grp = lane_c // C_GROUP
    win = jnp.where(grp == 0, s2, jnp.where(grp == 1, s4, jnp.where(grp == 2, s8, s16)))
    wsize = jnp.where(grp == 0, 2, jnp.where(grp == 1, 4, jnp.where(grp == 2, 8, 16)))
    tpos = t0 - HALO + lax.broadcasted_iota(jnp.int32, (n_ext, C_WIDTH), 0)
    cnt = jnp.minimum(tpos + 1, wsize).astype(F32)
    pooled = (win / jnp.maximum(cnt, 1.0) - zext)[HALO:]
    yc = jnp.dot(pooled.astype(BF16), wpool_ref[...], preferred_element_type=F32)
    yc_ref[...] = (yc * pscale_ref[...]).astype(BF16)

    cos = cos_ref[...]
    sin = sin_ref[...]
    cqn = (_rms(z[:, _C_CQ:_C_CKV]) * qn_ref[...]).astype(BF16)
    q = jnp.dot(cqn, wuq_ref[...], preferred_element_type=F32)
    ckvn = (_rms(z[:, _C_CKV:_C_ZP]) * kvn_ref[...]).astype(BF16)
    kn = jnp.dot(ckvn, wk_ref[...], preferred_element_type=F32)
    v_ref[...] = jnp.dot(ckvn, wv_ref[...], preferred_element_type=F32).astype(BF16)
    krope = _rope_apply(z[:, _C_KR:_C_END], cos, sin)
    qscale = 1.0 / math.sqrt(QK_DIM)
    for hd in range(MLA_HEADS):
        sl = slice(hd * HEAD_PAD, (hd + 1) * HEAD_PAD)
        q_ref[:, sl] = (_rope_apply(q[:, sl], cos, sin) * qscale).astype(BF16)
        k_ref[:, sl] = (kn[:, sl] + krope).astype(BF16)


def _inproj(x, mod, w1, w_s, bs_full, q_norm, wuq, kv_norm, wk, wv, cos, sin, wpool, pscale, *, seq, tb):
    T, D = x.shape
    hpb = tb // HALO
    kern = functools.partial(_inproj_kernel, seq=seq, tb=tb)
    full = lambda a: pl.BlockSpec(a.shape, lambda i: (0,) * a.ndim)
    hw = MLA_HEADS * HEAD_PAD
    return pl.pallas_call(
        kern,
        grid=(T // tb,),
        in_specs=[
            pl.BlockSpec((tb, D), lambda i: (i, 0)),
            pl.BlockSpec((HALO, D), lambda i: (jnp.maximum(i * hpb - 1, 0), 0)),
            pl.BlockSpec((1, SUBLANES, D), lambda i: ((i * tb) // seq, 0, 0)),
            full(w1), full(w_s), full(bs_full), full(q_norm), full(wuq), full(kv_norm),
            full(wk), full(wv),
            pl.BlockSpec((tb, LANES), lambda i: (i, 0)),
            pl.BlockSpec((tb, LANES), lambda i: (i, 0)),
            full(wpool), full(pscale),
        ],
        out_specs=[
            pl.BlockSpec((tb, A_WIDTH), lambda i: (i, 0)),
            pl.BlockSpec((tb, C_WIDTH), lambda i: (i, 0)),
            pl.BlockSpec((tb, hw), lambda i: (i, 0)),
            pl.BlockSpec((tb, hw), lambda i: (i, 0)),
            pl.BlockSpec((tb, hw), lambda i: (i, 0)),
        ],
        out_shape=[
            jax.ShapeDtypeStruct((T, A_WIDTH), BF16),
            jax.ShapeDtypeStruct((T, C_WIDTH), BF16),
            jax.ShapeDtypeStruct((T, hw), BF16),
            jax.ShapeDtypeStruct((T, hw), BF16),
            jax.ShapeDtypeStruct((T, hw), BF16),
        ],
        compiler_params=_cparams(("arbitrary",)),
        name="inproj",
    )(x, x, mod, w1, w_s, bs_full, q_norm, wuq, kv_norm, wk, wv, cos, sin, wpool, pscale)


def _attn_kernel(q_ref, k_ref, v_ref, o_ref, m_scr, l_scr, acc_scr, *, ta):
    qi = pl.program_id(2)
    ki = pl.program_id(3)

    @pl.when(ki == 0)
    def _():
        m_scr[...] = jnp.full(m_scr.shape, NEG, F32)
        l_scr[...] = jnp.zeros(l_scr.shape, F32)
        acc_scr[...] = jnp.zeros(acc_scr.shape, F32)

    def step(masked):
        s = lax.dot_general(q_ref[...], k_ref[...], (((1,), (1,)), ((), ())),
                            preferred_element_type=F32)
        if masked:
            row = lax.broadcasted_iota(jnp.int32, s.shape, 0)
            col = lax.broadcasted_iota(jnp.int32, s.shape, 1)
            s = jnp.where(col <= row, s, NEG)
        m_prev = m_scr[...]
        m_new = jnp.maximum(m_prev, jnp.max(s, axis=1, keepdims=True))
        alpha = jnp.exp(m_prev - m_new)
        p = jnp.exp(s - m_new[:, 0:1])
        l_scr[...] = alpha * l_scr[...] + jnp.sum(p, axis=1, keepdims=True)
        acc_scr[...] = alpha * acc_scr[...] + jnp.dot(p.astype(BF16), v_ref[...],
                                                      preferred_element_type=F32)
        m_scr[...] = m_new

    @pl.when(ki < qi)
    def _():
        step(False)

    @pl.when(ki == qi)
    def _():
        step(True)
        o_ref[...] = (acc_scr[...] / l_scr[...]).astype(BF16)


def _attention(q, k, v, *, batch, seq, ta):
    T, hw = q.shape
    nq = seq // ta
    kern = functools.partial(_attn_kernel, ta=ta)
    qspec = pl.BlockSpec((ta, HEAD_PAD), lambda b, h, qi, ki: (b * nq + qi, h))
    kspec = pl.BlockSpec((ta, HEAD_PAD), lambda b, h, qi, ki: (b * nq + jnp.minimum(ki, qi), h))
    return pl.pallas_call(
        kern,
        grid=(batch, MLA_HEADS, nq, nq),
        in_specs=[qspec, kspec, kspec],
        out_specs=qspec,
        out_shape=jax.ShapeDtypeStruct((T, hw), BF16),
        scratch_shapes=[pltpu.VMEM((ta, HEAD_PAD), F32)] * 3,
        compiler_params=_cparams(("arbitrary",) * 4),
        name="attention",
    )(q, k, v)


def _merge_kernel(x_ref, mod_ref, ya_ref, yb_ref, yc_ref, wg_ref, wa_ref, wb_ref, wc_ref, wo_ref,
                  xo_ref):
    D = D_MODEL
    x = x_ref[...]
    m = mod_ref[0]
    sh1, sc1, g1 = m[0:1], m[1:2], m[2:3]
    h = (_rms(x) * (1.0 + sc1) + sh1).astype(BF16)
    gates = jax.nn.sigmoid(jnp.dot(h, wg_ref[...], preferred_element_type=F32))
    merged = gates[:, 0:D] * jnp.dot(ya_ref[...], wa_ref[...], preferred_element_type=F32)
    merged += gates[:, D:2 * D] * jnp.dot(yb_ref[...], wb_ref[...], preferred_element_type=F32)
    merged += gates[:, 2 * D:3 * D] * jnp.dot(yc_ref[...], wc_ref[...], preferred_element_type=F32)
    xo_ref[...] = x + g1 * jnp.dot(merged.astype(BF16), wo_ref[...], preferred_element_type=F32)


def _merge(x, mod, ya, yb, yc, wg, wa, wb, wc, wo, *, seq, tb):
    T, D = x.shape
    full = lambda a: pl.BlockSpec(a.shape, lambda i: (0,) * a.ndim)
    row = lambda w: pl.BlockSpec((tb, w), lambda i: (i, 0))
    return pl.pallas_call(
        _merge_kernel,
        grid=(T // tb,),
        in_specs=[row(D), pl.BlockSpec((1, SUBLANES, D), lambda i: ((i * tb) // seq, 0, 0)),
                  row(ya.shape[1]), row(yb.shape[1]), row(yc.shape[1]),
                  full(wg), full(wa), full(wb), full(wc), full(wo)],
        out_specs=row(D),
        out_shape=jax.ShapeDtypeStruct((T, D), F32),
        compiler_params=_cparams(("arbitrary",)),
        name="merge",
    )(x, mod, ya, yb, yc, wg, wa, wb, wc, wo)


def _top16_rows(s, vals_ref, idx_ref, payload=None):
    rows = lax.broadcasted_iota(jnp.int32, s.shape, 0).astype(F32)
    n = float(s.shape[0])
    for j in range(PEER_TOPK):
        mx = jnp.max(s, axis=0, keepdims=True)
        pos = jnp.min(jnp.where(s == mx, rows, n), axis=0, keepdims=True)
        hit = rows == pos
        vals_ref[j:j + 1, :] = mx
        if payload is None:
            idx_ref[j:j + 1, :] = pos
        else:
            picked = jnp.sum(jnp.where(hit, payload, 0.0), axis=0, keepdims=True)
            idx_ref[j:j + 1, :] = picked.astype(jnp.int32)
        s = jnp.where(hit, NEG, s)


def _route_kernel(x_ref, mod_ref, wpq_ref, keys_ref, h2_ref, idx_ref, g_ref,
                  h2_scr, v1_scr, i1_scr, v2_scr, i2_scr, tv_scr):
    hd = pl.program_id(1)

    @pl.when(hd == 0)
    def _():
        m = mod_ref[0]
        sh2, sc2 = m[3:4], m[4:5]
        h2 = _rms(x_ref[...]) * (1.0 + sc2) + sh2
        h2_ref[...] = h2
        h2_scr[...] = h2.astype(BF16)

    q = jnp.dot(h2_scr[...], wpq_ref[...], preferred_element_type=F32).astype(BF16)
    nt = (((1,), (1,)), ((), ()))
    s1 = lax.dot_general(keys_ref[0, 0], q[:, :PEER_HALF], nt, preferred_element_type=F32)
    s2 = lax.dot_general(keys_ref[0, 1], q[:, PEER_HALF:], nt, preferred_element_type=F32)
    _top16_rows(s1, v1_scr, i1_scr)
    _top16_rows(s2, v2_scr, i2_scr)
    v1, v2 = v1_scr[...], v2_scr[...]
    i1, i2 = i1_scr[...] * N_KEYS, i2_scr[...]

    r8 = lax.broadcasted_iota(jnp.int32, (SUBLANES, v1.shape[1]), 0)
    low = r8 < 4

    def two(a0, a1, nb0, nb1, val, idx):
        va = jnp.where(low, val[a0:a0 + 1], val[a1:a1 + 1])
        ia = jnp.where(low, idx[a0:a0 + 1], idx[a1:a1 + 1])
        ok = r8 < jnp.where(low, nb0, nb1 + 4)
        return va, ia, ok

    v2lo, i2lo = v2[0:SUBLANES], i2[0:SUBLANES]
    v2rep = jnp.where(low, v2lo, pltpu.roll(v2lo, 4, axis=0))
    i2rep = jnp.where(low, i2lo, pltpu.roll(i2lo, 4, axis=0))
    cand = [v1[0:1] + v2[0:SUBLANES], v1[0:1] + v2[SUBLANES:], v1[1:2] + v2lo]
    cidx = [i1[0:1] + i2[0:SUBLANES], i1[0:1] + i2[SUBLANES:], i1[1:2] + i2lo]
    for a, nb in ((2, 5), (3, 4)):
        cand.append(jnp.where(r8 < nb, v1[a:a + 1] + v2lo, NEG))
        cidx.append(i1[a:a + 1] + i2lo)
    for a0, a1, nb0, nb1 in ((4, 5, 3, 2), (6, 7, 2, 2)):
        va, ia, ok = two(a0, a1, nb0, nb1, v1, i1)
        cand.append(jnp.where(ok, va + v2rep, NEG))
        cidx.append(ia + i2rep)
    cand.append(v1[SUBLANES:] + v2[0:1])
    cidx.append(i1[SUBLANES:] + i2[0:1])
    cand = jnp.concatenate(cand, axis=0)
    cidx = jnp.concatenate(cidx, axis=0)
    _top16_rows(cand, tv_scr, idx_ref, payload=cidx)
    tv = tv_scr[...]
    e = jnp.exp(tv - tv[0:1])
    g_ref[...] = e / jnp.sum(e, axis=0, keepdims=True)


def _route(x, mod, wpq, keys, *, seq, tb):
    T, D = x.shape
    kd = 2 * PEER_HALF
    return pl.pallas_call(
        _route_kernel,
        grid=(T // tb, PEER_HEADS),
        in_specs=[
            pl.BlockSpec((tb, D), lambda i, h: (i, 0)),
            pl.BlockSpec((1, SUBLANES, D), lambda i, h: ((i * tb) // seq, 0, 0)),
            pl.BlockSpec((D, kd), lambda i, h: (0, h)),
            pl.BlockSpec((1, 2, N_KEYS, PEER_HALF), lambda i, h: (h, 0, 0, 0)),
        ],
        out_specs=[
            pl.BlockSpec((tb, D), lambda i, h: (i, 0)),
            pl.BlockSpec((PEER_TOPK, tb), lambda i, h: (h, i)),
            pl.BlockSpec((PEER_TOPK, tb), lambda i, h: (h, i)),
        ],
        out_shape=[
            jax.ShapeDtypeStruct((T, D), F32),
            jax.ShapeDtypeStruct((N_PAIRS, T), jnp.int32),
            jax.ShapeDtypeStruct((N_PAIRS, T), F32),
        ],
        scratch_shapes=[
            pltpu.VMEM((tb, D), BF16),
            pltpu.VMEM((PEER_TOPK, tb), F32), pltpu.VMEM((PEER_TOPK, tb), F32),
            pltpu.VMEM((PEER_TOPK, tb), F32), pltpu.VMEM((PEER_TOPK, tb), F32),
            pltpu.VMEM((PEER_TOPK, tb), F32),
        ],
        compiler_params=_cparams(("arbitrary", "arbitrary")),
        name="route",
    )(x, mod, wpq, keys)


def _load_table(tab_hbm, tab_vmem, sem):
    @pl.when(pl.program_id(0) == 0)
    def _():
        cp = pltpu.make_async_copy(tab_hbm, tab_vmem, sem)
        cp.start()
        cp.wait()


def _expert_row(tab_vmem, e):
    w = tab_vmem[e & (HALF_EXPERTS - 1)]
    sh = ((e >> 13) << 4).astype(jnp.uint32)
    return pltpu.bitcast((w << sh) & jnp.uint32(0xFFFF0000), F32)


def _fold(parts, shift):
    sub = lax.broadcasted_iota(jnp.int32, parts[0].shape, 0)
    first = (sub % (2 * shift)) < shift
    out = []
    for a, b in zip(parts[0::2], parts[1::2]):
        fa = a + pltpu.roll(a, SUBLANES - shift, axis=0)
        fb = b + pltpu.roll(b, shift, axis=0)
        out.append(jnp.where(first, fa, fb))
    return out


def _fold_positions():
    def fold(parts, shift):
        sub = np.arange(SUBLANES)[:, None]
        first = (sub % (2 * shift)) < shift
        return [np.where(first, a + np.roll(a, SUBLANES - shift, axis=0), b + np.roll(b, shift, axis=0))
                for a, b in zip(parts[0::2], parts[1::2])]
    parts = [np.full((SUBLANES, 1), float(1 << j) / SUBLANES) for j in range(SUBLANES)]
    for shift in (4, 2, 1):
        parts = fold(parts, shift)
    return [int(round(math.log2(v))) for v in parts[0][:, 0]]


_FOLD_POSITIONS = _fold_positions()


def _sublane_sums(tiles):
    parts = [None] * SUBLANES
    for sublane, pos in enumerate(_FOLD_POSITIONS):
        parts[pos] = tiles[sublane]
    for shift in (4, 2, 1):
        parts = _fold(parts, shift)
    return parts[0]


def _peer_u_kernel(idx_ref, h_ref, g_ref, tab_hbm, coef_ref, tab_vmem, act_scr, sem, *, tb):
    _load_table(tab_hbm, tab_vmem, sem)
    lane = lax.broadcasted_iota(jnp.int32, (N_PAIRS, tb), 1)

    def token(t, carry):
        xt = h_ref[t]
        groups = []
        for g0 in range(0, N_PAIRS, SUBLANES):
            prods = [_expert_row(tab_vmem, idx_ref[g0 + j, t]) * xt for j in range(SUBLANES)]
            groups.append(_sublane_sums(prods))
        a = jnp.sum(jnp.concatenate(groups, axis=0), axis=1, keepdims=True)
        act_scr[...] = jnp.where(lane == t, a, act_scr[...])
        return carry

    lax.fori_loop(0, tb, token, 0)
    coef_ref[...] = g_ref[...] * jax.nn.gelu(act_scr[...])


def _peer_u(idx_t, h3, g_t, tab, *, tb):
    T = h3.shape[0]
    kern = functools.partial(_peer_u_kernel, tb=tb)
    return pl.pallas_call(
        kern,
        grid=(T // tb,),
        in_specs=[
            pl.BlockSpec((N_PAIRS, tb), lambda i: (0, i), memory_space=pltpu.SMEM),
            pl.BlockSpec((tb, SUBLANES, LANES), lambda i: (i, 0, 0)),
            pl.BlockSpec((N_PAIRS, tb), lambda i: (0, i)),
            pl.BlockSpec(memory_space=pl.ANY),
        ],
        out_specs=pl.BlockSpec((N_PAIRS, tb), lambda i: (0, i)),
        out_shape=jax.ShapeDtypeStruct((N_PAIRS, T), F32),
        scratch_shapes=[
            pltpu.VMEM((HALF_EXPERTS, SUBLANES, LANES), jnp.uint32),
            pltpu.VMEM((N_PAIRS, tb), F32),
            pltpu.SemaphoreType.DMA,
        ],
        compiler_params=_cparams(("arbitrary",), TABLE_VMEM_LIMIT),
        name="peer_u",
    )(idx_t, h3, g_t, tab)


def _peer_v_kernel(idx_ref, coef_ref, x_ref, g2_ref, tab_hbm, xo_ref, tab_vmem, sem, *, tb):
    _load_table(tab_hbm, tab_vmem, sem)
    g2 = g2_ref[0]
    n_acc = 4

    def token(t, carry):
        acc = [jnp.zeros((SUBLANES, LANES), F32) for _ in range(n_acc)]
        for k in range(N_PAIRS):
            acc[k % n_acc] = acc[k % n_acc] + coef_ref[k, t] * _expert_row(tab_vmem, idx_ref[k, t])
        xo_ref[t] = x_ref[t] + g2 * ((acc[0] + acc[1]) + (acc[2] + acc[3]))
        return carry

    lax.fori_loop(0, tb, token, 0)


def _peer_v(idx_t, coef_t, x3, g2, tab, *, seq, tb):
    T = x3.shape[0]
    kern = functools.partial(_peer_v_kernel, tb=tb)
    tok = pl.BlockSpec((tb, SUBLANES, LANES), lambda i: (i, 0, 0))
    return pl.pallas_call(
        kern,
        grid=(T // tb,),
        in_specs=[
            pl.BlockSpec((N_PAIRS, tb), lambda i: (0, i), memory_space=pltpu.SMEM),
            pl.BlockSpec((N_PAIRS, tb), lambda i: (0, i), memory_space=pltpu.SMEM),
            tok,
            pl.BlockSpec((1, SUBLANES, LANES), lambda i: ((i * tb) // seq, 0, 0)),
            pl.BlockSpec(memory_space=pl.ANY),
        ],
        out_specs=tok,
        out_shape=jax.ShapeDtypeStruct(x3.shape, F32),
        scratch_shapes=[
            pltpu.VMEM((HALF_EXPERTS, SUBLANES, LANES), jnp.uint32),
            pltpu.SemaphoreType.DMA,
        ],
        compiler_params=_cparams(("arbitrary",), TABLE_VMEM_LIMIT),
        name="peer_v",
    )(idx_t, coef_t, x3, g2, tab)


def _final_kernel(x_ref, g_ref, o_ref):
    o_ref[...] = _rms(x_ref[...]) * g_ref[...]


def _final_norm(x, gain, tb):
    T, D = x.shape
    return pl.pallas_call(
        _final_kernel,
        grid=(T // tb,),
        in_specs=[pl.BlockSpec((tb, D), lambda i: (i, 0)), pl.BlockSpec((1, D), lambda i: (0, 0))],
        out_specs=pl.BlockSpec((tb, D), lambda i: (i, 0)),
        out_shape=jax.ShapeDtypeStruct((T, D), F32),
        compiler_params=_cparams(("arbitrary",)),
        name="final_norm",
    )(x, gain)


def _pack_table(tab):
    bits = lax.bitcast_convert_type(tab.astype(BF16), jnp.uint16).astype(jnp.uint32)
    packed = (bits[:HALF_EXPERTS] << 16) | bits[HALF_EXPERTS:]
    return packed.reshape(HALF_EXPERTS, SUBLANES, LANES)


def _pad_heads(w, per_head, offset):
    k = w.shape[0]
    w = w.reshape(k, MLA_HEADS, per_head)
    out = jnp.zeros((k, MLA_HEADS, HEAD_PAD), w.dtype)
    out = out.at[:, :, offset:offset + per_head].set(w)
    return out.reshape(k, MLA_HEADS * HEAD_PAD)


def _block_size(seq, want):
    tb = min(want, seq)
    assert seq % tb == 0
    return tb


def _layer(xf, mod_l, cos, sin, w_in, w_s, b_s, q_norm, w_uq, kv_norm, w_ukv, w_pool, pool_scale,
           w_a, w_b, w_c, w_o, w_pq, sub_keys, u_tab, v_tab, *, batch, seq):
    T, D = xf.shape
    tb_in = _block_size(seq, 512)
    tb_mg = _block_size(seq, 256)
    tb_rt = _block_size(seq, 256)
    ta = _block_size(seq, 1024)
    tb_peer = LANES

    sp = (0, 256, 512, 896, 1152, 1184, 1440)
    kr_pad = jnp.zeros((D, LANES), F32).at[:, QK_NOPE:QK_NOPE + QK_ROPE].set(w_in[:, sp[4]:sp[5]])
    w1 = jnp.concatenate([w_in[:, :sp[4]], w_in[:, sp[5]:sp[6]], kr_pad], axis=1).astype(BF16)
    wg = w_in[:, sp[6]:].astype(BF16)
    bs_full = jnp.repeat(b_s.T, A_GROUP_DIM, axis=1)
    wuq = _pad_heads(w_uq, QK_DIM, 0).astype(BF16)
    wukv = w_ukv.reshape(KV_LORA, MLA_HEADS, QK_NOPE + V_DIM)
    wk = _pad_heads(wukv[:, :, :QK_NOPE].reshape(KV_LORA, -1), QK_NOPE, 0).astype(BF16)
    wv = _pad_heads(wukv[:, :, QK_NOPE:].reshape(KV_LORA, -1), V_DIM, 0).astype(BF16)
    wpool = jax.scipy.linalg.block_diag(*[w_pool[g] for g in range(len(POOL_WINDOWS))]).astype(BF16)
    wb = w_b.reshape(MLA_HEADS, V_DIM, D)
    wb = jnp.pad(wb, ((0, 0), (0, HEAD_PAD - V_DIM), (0, 0))).reshape(MLA_HEADS * HEAD_PAD, D).astype(BF16)

    ya, yc, q, k, v = _inproj(
        xf, mod_l, w1, w_s, bs_full, q_norm.reshape(1, -1), wuq, kv_norm.reshape(1, -1),
        wk, wv, cos, sin, wpool, pool_scale.reshape(1, -1), seq=seq, tb=tb_in)
    yb = _attention(q, k, v, batch=batch, seq=seq, ta=ta)
    x1 = _merge(xf, mod_l, ya, yb, yc, wg, w_a.astype(BF16), wb, w_c.astype(BF16),
                w_o.astype(BF16), seq=seq, tb=tb_mg)
    h2, idx_t, g_t = _route(x1, mod_l, w_pq.astype(BF16), sub_keys.astype(BF16), seq=seq, tb=tb_rt)
    coef_t = _peer_u(idx_t, h2.reshape(T, SUBLANES, LANES), g_t, _pack_table(u_tab), tb=tb_peer)
    g2 = mod_l[:, 5].reshape(batch, SUBLANES, LANES)
    x2 = _peer_v(idx_t, coef_t, x1.reshape(T, SUBLANES, LANES), g2, _pack_table(v_tab),
                 seq=seq, tb=tb_peer)
    return x2.reshape(T, D), (x1, h2)


def kernel(x, c, positions, w_mod, b_mod, w_in, w_s, b_s, q_norm, w_uq, kv_norm, w_ukv, w_pool, pool_scale, w_a, w_b, w_c, w_o, w_pq, sub_keys, u_tab, v_tab, final_norm):
    B, S, D = x.shape
    L = w_mod.shape[0]
    T = B * S
    assert D == D_MODEL and S % A_CHUNK == 0

    c_pad = jnp.zeros((SUBLANES, D), F32).at[:B].set(c)
    mod = _modulation(c_pad, w_mod, b_mod)[:, :B].reshape(L, B, 6, D)
    mod = jnp.pad(mod, ((0, 0), (0, 0), (0, SUBLANES - 6), (0, 0)))

    inv_freq = 1.0 / (ROPE_THETA ** (jnp.arange(0, QK_ROPE, 2, dtype=F32) / QK_ROPE))
    half = QK_ROPE // 2
    freq_lane = jnp.zeros((LANES,), F32)
    freq_lane = freq_lane.at[QK_NOPE:QK_NOPE + half].set(inv_freq)
    freq_lane = freq_lane.at[QK_NOPE + half:QK_NOPE + QK_ROPE].set(inv_freq)
    ang = positions.astype(F32).reshape(T, 1) * freq_lane[None, :]
    cos, sin = _rope_tables(ang, _block_size(T, 1024))

    xf = x.reshape(T, D)
    for l in range(L):
        xf, _ = _layer(xf, mod[l], cos, sin, w_in[l], w_s[l], b_s[l], q_norm[l], w_uq[l], kv_norm[l],
                       w_ukv[l], w_pool[l], pool_scale[l], w_a[l], w_b[l], w_c[l], w_o[l], w_pq[l],
                       sub_keys[l], u_tab[l], v_tab[l], batch=B, seq=S)

    out = _final_norm(xf, final_norm.reshape(1, D), _block_size(T, 1024))
    return out.reshape(B, S, D)
```

```python
import functools
import math

import numpy as np
import jax
import jax.numpy as jnp
from jax import lax
from jax.experimental import pallas as pl
from jax.experimental.pallas import tpu as pltpu

F32 = jnp.float32
BF16 = jnp.bfloat16

EPS = 1e-6
D_MODEL = 1024
A_WIDTH = 256
A_GROUPS = 4
A_GROUP_DIM = A_WIDTH // A_GROUPS
A_CHUNK = 128
MLA_HEADS = 8
Q_LORA = 384
KV_LORA = 256
QK_NOPE = 64
QK_ROPE = 32
V_DIM = 64
QK_DIM = QK_NOPE + QK_ROPE
ROPE_THETA = 10000.0
POOL_WINDOWS = (2, 4, 8, 16)
C_WIDTH = 256
C_GROUP = C_WIDTH // len(POOL_WINDOWS)
PEER_HEADS = 8
N_KEYS = 128
N_EXPERTS = N_KEYS * N_KEYS
PEER_HALF = 128
PEER_TOPK = 16
N_PAIRS = PEER_HEADS * PEER_TOPK

LANES = 128
SUBLANES = 8
HEAD_PAD = LANES
HALO = 16
HALF_EXPERTS = N_EXPERTS // 2
NEG = -1e30
ATT_CHUNKS = 1
ONES_LANE = V_DIM

VMEM_LIMIT = 48 * 1024 * 1024
TABLE_VMEM_LIMIT = 56 * 1024 * 1024

_C_ZU, _C_ZV, _C_CQ, _C_CKV, _C_ZP, _C_KR, _C_END = 0, 256, 512, 896, 1152, 1408, 1536


def _rms(x):
    return x * lax.rsqrt(jnp.mean(x * x, axis=-1, keepdims=True) + EPS)


def _cparams(sem, limit=VMEM_LIMIT):
    return pltpu.CompilerParams(dimension_semantics=sem, vmem_limit_bytes=limit)


def _mod_kernel(c_ref, w_ref, b_ref, o_ref):
    c = c_ref[...]
    ca = c * jax.nn.sigmoid(c)
    o_ref[0] = jnp.dot(ca, w_ref[0], preferred_element_type=F32) + b_ref[0]


def _modulation(c_pad, w_mod, b_mod):
    L, D, six_d = w_mod.shape
    nb = six_d // D
    rows = c_pad.shape[0]
    return pl.pallas_call(
        _mod_kernel,
        grid=(L, nb),
        in_specs=[
            pl.BlockSpec((rows, D), lambda l, j: (0, 0)),
            pl.BlockSpec((1, D, D), lambda l, j: (l, 0, j)),
            pl.BlockSpec((1, 1, D), lambda l, j: (l, 0, j)),
        ],
        out_specs=pl.BlockSpec((1, rows, D), lambda l, j: (l, 0, j)),
        out_shape=jax.ShapeDtypeStruct((L, rows, six_d), F32),
        compiler_params=_cparams(("arbitrary", "arbitrary")),
        name="modulation",
    )(c_pad, w_mod, b_mod.reshape(L, 1, six_d))


def _rope_kernel(ang_ref, cos_ref, sin_ref):
    ang = ang_ref[...]
    lane = lax.broadcasted_iota(jnp.int32, ang.shape, 1)
    first_half = lane < QK_NOPE + QK_ROPE // 2
    cos_ref[...] = jnp.cos(ang)
    s = jnp.sin(ang)
    sin_ref[...] = jnp.where(first_half, -s, s)


def _rope_tables(ang, tb):
    T = ang.shape[0]
    spec = pl.BlockSpec((tb, LANES), lambda i: (i, 0))
    return pl.pallas_call(
        _rope_kernel,
        grid=(T // tb,),
        in_specs=[spec],
        out_specs=[spec, spec],
        out_shape=[jax.ShapeDtypeStruct((T, LANES), F32)] * 2,
        compiler_params=_cparams(("arbitrary",)),
        name="rope_tables",
    )(ang)


def _rope_apply(xh, cos, sin):
    lane = lax.broadcasted_iota(jnp.int32, xh.shape, 1)
    half = QK_ROPE // 2
    partner = jnp.where(lane < QK_NOPE + half,
                        pltpu.roll(xh, LANES - half, axis=1),
                        pltpu.roll(xh, half, axis=1))
    return xh * cos + partner * sin


def _inproj_kernel(x_ref, xh_ref, mod_ref, w1_ref, ws_ref, bs_ref, qn_ref, wuq_ref, kvn_ref,
                   wk_ref, wv_ref, cos_ref, sin_ref, wpool_ref, pscale_ref,
                   ya_ref, yc_ref, q_ref, k_ref, v_ref, *, seq, tb):
    i = pl.program_id(0)
    m = mod_ref[0]
    sh1, sc1 = m[0:1], m[1:2]
    h = (_rms(x_ref[...]) * (1.0 + sc1) + sh1).astype(BF16)
    z = jnp.dot(h, w1_ref[...], preferred_element_type=F32)

    gu = jax.nn.gelu(z[:, _C_ZU:_C_ZV])
    gv = jax.nn.gelu(z[:, _C_ZV:_C_CQ])
    mu = jnp.mean(gv, axis=-1, keepdims=True)
    dv = gv - mu
    vn = (dv * lax.rsqrt(jnp.mean(dv * dv, axis=-1, keepdims=True) + EPS)).astype(BF16)
    r = lax.broadcasted_iota(jnp.int32, (A_CHUNK, A_CHUNK), 0)
    cc = lax.broadcasted_iota(jnp.int32, (A_CHUNK, A_CHUNK), 1)
    lane_a = lax.broadcasted_iota(jnp.int32, (A_CHUNK, A_WIDTH), 1)
    wtril = [jnp.where(r >= cc, ws_ref[g], 0.0).astype(BF16) for g in range(A_GROUPS)]
    for ci in range(tb // A_CHUNK):
        lo = ci * A_CHUNK
        vc = vn[lo:lo + A_CHUNK]
        mixed = bs_ref[...]
        for g in range(A_GROUPS):
            dg = jnp.dot(wtril[g], vc, preferred_element_type=F32)
            in_g = (lane_a >= g * A_GROUP_DIM) & (lane_a < (g + 1) * A_GROUP_DIM)
            mixed = mixed + jnp.where(in_g, dg, 0.0)
        ya_ref[lo:lo + A_CHUNK, :] = (gu[lo:lo + A_CHUNK] * mixed).astype(BF16)

    zp = z[:, _C_ZP:_C_KR]
    t0 = (i * tb) % seq
    hh = (_rms(xh_ref[...]) * (1.0 + sc1) + sh1).astype(BF16)
    zph = jnp.dot(hh, w1_ref[:, _C_ZP:_C_KR], preferred_element_type=F32)
    zph = zph * jnp.where(t0 == 0, 0.0, 1.0)
    zext = jnp.concatenate([zph, zp], axis=0)
    n_ext = HALO + tb
    s2 = zext + pltpu.roll(zext, 1, axis=0)
    s4 = s2 + pltpu.roll(s2, 2, axis=0)
    s8 = s4 + pltpu.roll(s4, 4, axis=0)
    s16 = s8 + pltpu.roll(s8, 8, axis=0)
    lane_c = lax.broadcasted_iota(jnp.int32, (n_ext, C_WIDTH), 1)
    grp = lane_c // C_GROUP
    win = jnp.where(grp == 0, s2, jnp.where(grp == 1, s4, jnp.where(grp == 2, s8, s16)))
    wsize = jnp.where(grp == 0, 2, jnp.where(grp == 1, 4, jnp.where(grp == 2, 8, 16)))
    tpos = t0 - HALO + lax.broadcasted_iota(jnp.int32, (n_ext, C_WIDTH), 0)
    cnt = jnp.minimum(tpos + 1, wsize).astype(F32)
    pooled = (win / jnp.maximum(cnt, 1.0) - zext)[HALO:]
    yc = jnp.dot(pooled.astype(BF16), wpool_ref[...], preferred_element_type=F32)
    yc_ref[...] = (yc * pscale_ref[...]).astype(BF16)

    cos = cos_ref[...]
    sin = sin_ref[...]
    cqn = (_rms(z[:, _C_CQ:_C_CKV]) * qn_ref[...]).astype(BF16)
    q = jnp.dot(cqn, wuq_ref[...], preferred_element_type=F32)
    ckvn = (_rms(z[:, _C_CKV:_C_ZP]) * kvn_ref[...]).astype(BF16)
    kn = jnp.dot(ckvn, wk_ref[...], preferred_element_type=F32)
    vv = jnp.dot(ckvn, wv_ref[...], preferred_element_type=F32)
    lane_v = lax.broadcasted_iota(jnp.int32, vv.shape, 1)
    v_ref[...] = jnp.where(lane_v % HEAD_PAD == ONES_LANE, 1.0, vv).astype(BF16)
    krope = _rope_apply(z[:, _C_KR:_C_END], cos, sin)
    qscale = math.log2(math.e) / math.sqrt(QK_DIM)
    for hd in range(MLA_HEADS):
        sl = slice(hd * HEAD_PAD, (hd + 1) * HEAD_PAD)
        q_ref[:, sl] = (_rope_apply(q[:, sl], cos, sin) * qscale).astype(BF16)
        k_ref[:, sl] = (kn[:, sl] + krope).astype(BF16)


def _inproj(x, mod, w1, w_s, bs_full, q_norm, wuq, kv_norm, wk, wv, cos, sin, wpool, pscale, *, seq, tb):
    T, D = x.shape
    hpb = tb // HALO
    kern = functools.partial(_inproj_kernel, seq=seq, tb=tb)
    full = lambda a: pl.BlockSpec(a.shape, lambda i: (0,) * a.ndim)
    hw = MLA_HEADS * HEAD_PAD
    return pl.pallas_call(
        kern,
        grid=(T // tb,),
        in_specs=[
            pl.BlockSpec((tb, D), lambda i: (i, 0)),
            pl.BlockSpec((HALO, D), lambda i: (jnp.maximum(i * hpb - 1, 0), 0)),
            pl.BlockSpec((1, SUBLANES, D), lambda i: ((i * tb) // seq, 0, 0)),
            full(w1), full(w_s), full(bs_full), full(q_norm), full(wuq), full(kv_norm),
            full(wk), full(wv),
            pl.BlockSpec((tb, LANES), lambda i: (i, 0)),
            pl.BlockSpec((tb, LANES), lambda i: (i, 0)),
            full(wpool), full(pscale),
        ],
        out_specs=[
            pl.BlockSpec((tb, A_WIDTH), lambda i: (i, 0)),
            pl.BlockSpec((tb, C_WIDTH), lambda i: (i, 0)),
            pl.BlockSpec((tb, hw), lambda i: (i, 0)),
            pl.BlockSpec((tb, hw), lambda i: (i, 0)),
            pl.BlockSpec((tb, hw), lambda i: (i, 0)),
        ],
        out_shape=[
            jax.ShapeDtypeStruct((T, A_WIDTH), BF16),
            jax.ShapeDtypeStruct((T, C_WIDTH), BF16),
            jax.ShapeDtypeStruct((T, hw), BF16),
            jax.ShapeDtypeStruct((T, hw), BF16),
            jax.ShapeDtypeStruct((T, hw), BF16),
        ],
        compiler_params=_cparams(("arbitrary",)),
        name="inproj",
    )(x, x, mod, w1, w_s, bs_full, q_norm, wuq, kv_norm, wk, wv, cos, sin, wpool, pscale)


def _attn_kernel(q_ref, kt_ref, v_ref, o_ref, m_scr, acc_scr, *, ta):
    qi = pl.program_id(2)
    ki = pl.program_id(3)

    @pl.when(ki == 0)
    def _():
        m_scr[...] = jnp.full(m_scr.shape, NEG, F32)
        acc_scr[...] = jnp.zeros(acc_scr.shape, F32)

    def block(diagonal):
        n_chunks = ATT_CHUNKS if ta % (ATT_CHUNKS * SUBLANES) == 0 else 1
        rows = ta // n_chunks
        results = []
        for h in range(n_chunks):
            sl = slice(h * rows, (h + 1) * rows)
            s = jnp.dot(q_ref[sl, :], kt_ref[...], preferred_element_type=F32)
            if diagonal:
                row = h * rows + lax.broadcasted_iota(jnp.int32, s.shape, 0)
                col = lax.broadcasted_iota(jnp.int32, s.shape, 1)
                s = jnp.where(col <= row, s, NEG)
            m_prev = m_scr[sl, :]
            m_new = jnp.maximum(m_prev, jnp.max(s, axis=1, keepdims=True))
            alpha = jnp.exp2(m_prev - m_new)
            p = jnp.exp2(s - m_new[:, 0:1]).astype(BF16)
            acc = alpha * acc_scr[sl, :] + jnp.dot(p, v_ref[...], preferred_element_type=F32)
            results.append((sl, m_new, acc))
        for sl, m_new, acc in results:
            m_scr[sl, :] = m_new
            acc_scr[sl, :] = acc

    @pl.when(ki < qi)
    def _():
        block(False)

    @pl.when(ki == qi)
    def _():
        block(True)
        acc = acc_scr[...]
        o_ref[...] = (acc / acc[:, ONES_LANE:ONES_LANE + 1]).astype(BF16)


def _attention(q, kt, v, *, batch, seq, ta):
    T, hw = q.shape
    nq = seq // ta
    kern = functools.partial(_attn_kernel, ta=ta)
    qspec = pl.BlockSpec((ta, HEAD_PAD), lambda b, h, qi, ki: (b * nq + qi, h))
    kspec = pl.BlockSpec((ta, HEAD_PAD), lambda b, h, qi, ki: (b * nq + jnp.minimum(ki, qi), h))
    ktspec = pl.BlockSpec((HEAD_PAD, ta), lambda b, h, qi, ki: (h, b * nq + jnp.minimum(ki, qi)))
    return pl.pallas_call(
        kern,
        grid=(batch, MLA_HEADS, nq, nq),
        in_specs=[qspec, ktspec, kspec],
        out_specs=qspec,
        out_shape=jax.ShapeDtypeStruct((T, hw), BF16),
        scratch_shapes=[pltpu.VMEM((ta, HEAD_PAD), F32)] * 2,
        compiler_params=_cparams(("arbitrary",) * 4),
        name="attention",
    )(q, kt, v)


def _merge_kernel(x_ref, mod_ref, ya_ref, yb_ref, yc_ref, wg_ref, wa_ref, wb_ref, wc_ref, wo_ref,
                  xo_ref):
    D = D_MODEL
    x = x_ref[...]
    m = mod_ref[0]
    sh1, sc1, g1 = m[0:1], m[1:2], m[2:3]
    h = (_rms(x) * (1.0 + sc1) + sh1).astype(BF16)
    gates = jax.nn.sigmoid(jnp.dot(h, wg_ref[...], preferred_element_type=F32))
    merged = gates[:, 0:D] * jnp.dot(ya_ref[...], wa_ref[...], preferred_element_type=F32)
    merged += gates[:, D:2 * D] * jnp.dot(yb_ref[...], wb_ref[...], preferred_element_type=F32)
    merged += gates[:, 2 * D:3 * D] * jnp.dot(yc_ref[...], wc_ref[...], preferred_element_type=F32)
    xo_ref[...] = x + g1 * jnp.dot(merged.astype(BF16), wo_ref[...], preferred_element_type=F32)


def _merge(x, mod, ya, yb, yc, wg, wa, wb, wc, wo, *, seq, tb):
    T, D = x.shape
    full = lambda a: pl.BlockSpec(a.shape, lambda i: (0,) * a.ndim)
    row = lambda w: pl.BlockSpec((tb, w), lambda i: (i, 0))
    return pl.pallas_call(
        _merge_kernel,
        grid=(T // tb,),
        in_specs=[row(D), pl.BlockSpec((1, SUBLANES, D), lambda i: ((i * tb) // seq, 0, 0)),
                  row(ya.shape[1]), row(yb.shape[1]), row(yc.shape[1]),
                  full(wg), full(wa), full(wb), full(wc), full(wo)],
        out_specs=row(D),
        out_shape=jax.ShapeDtypeStruct((T, D), F32),
        compiler_params=_cparams(("arbitrary",)),
        name="merge",
    )(x, mod, ya, yb, yc, wg, wa, wb, wc, wo)


def _top16_rows(s, vals_ref, idx_ref, payload=None):
    rows = lax.broadcasted_iota(jnp.int32, s.shape, 0).astype(F32)
    n = float(s.shape[0])
    for j in range(PEER_TOPK):
        mx = jnp.max(s, axis=0, keepdims=True)
        pos = jnp.min(jnp.where(s == mx, rows, n), axis=0, keepdims=True)
        hit = rows == pos
        vals_ref[j:j + 1, :] = mx
        if payload is None:
            idx_ref[j:j + 1, :] = pos
        else:
            picked = jnp.sum(jnp.where(hit, payload, 0.0), axis=0, keepdims=True)
            idx_ref[j:j + 1, :] = picked.astype(jnp.int32)
        s = jnp.where(hit, NEG, s)


def _route_kernel(x_ref, mod_ref, wpq_ref, keys_ref, h2_ref, idx_ref, g_ref,
                  h2_scr, v1_scr, i1_scr, v2_scr, i2_scr, tv_scr):
    hd = pl.program_id(1)

    @pl.when(hd == 0)
    def _():
        m = mod_ref[0]
        sh2, sc2 = m[3:4], m[4:5]
        h2 = _rms(x_ref[...]) * (1.0 + sc2) + sh2
        h2_ref[...] = h2
        h2_scr[...] = h2.astype(BF16)

    q = jnp.dot(h2_scr[...], wpq_ref[...], preferred_element_type=F32).astype(BF16)
    nt = (((1,), (1,)), ((), ()))
    s1 = lax.dot_general(keys_ref[0, 0], q[:, :PEER_HALF], nt, preferred_element_type=F32)
    s2 = lax.dot_general(keys_ref[0, 1], q[:, PEER_HALF:], nt, preferred_element_type=F32)
    _top16_rows(s1, v1_scr, i1_scr)
    _top16_rows(s2, v2_scr, i2_scr)
    v1, v2 = v1_scr[...], v2_scr[...]
    i1, i2 = i1_scr[...] * N_KEYS, i2_scr[...]

    r8 = lax.broadcasted_iota(jnp.int32, (SUBLANES, v1.shape[1]), 0)
    low = r8 < 4

    def two(a0, a1, nb0, nb1, val, idx):
        va = jnp.where(low, val[a0:a0 + 1], val[a1:a1 + 1])
        ia = jnp.where(low, idx[a0:a0 + 1], idx[a1:a1 + 1])
        ok = r8 < jnp.where(low, nb0, nb1 + 4)
        return va, ia, ok

    v2lo, i2lo = v2[0:SUBLANES], i2[0:SUBLANES]
    v2rep = jnp.where(low, v2lo, pltpu.roll(v2lo, 4, axis=0))
    i2rep = jnp.where(low, i2lo, pltpu.roll(i2lo, 4, axis=0))
    cand = [v1[0:1] + v2[0:SUBLANES], v1[0:1] + v2[SUBLANES:], v1[1:2] + v2lo]
    cidx = [i1[0:1] + i2[0:SUBLANES], i1[0:1] + i2[SUBLANES:], i1[1:2] + i2lo]
    for a, nb in ((2, 5), (3, 4)):
        cand.append(jnp.where(r8 < nb, v1[a:a + 1] + v2lo, NEG))
        cidx.append(i1[a:a + 1] + i2lo)
    for a0, a1, nb0, nb1 in ((4, 5, 3, 2), (6, 7, 2, 2)):
        va, ia, ok = two(a0, a1, nb0, nb1, v1, i1)
        cand.append(jnp.where(ok, va + v2rep, NEG))
        cidx.append(ia + i2rep)
    cand.append(v1[SUBLANES:] + v2[0:1])
    cidx.append(i1[SUBLANES:] + i2[0:1])
    cand = jnp.concatenate(cand, axis=0)
    cidx = jnp.concatenate(cidx, axis=0)
    _top16_rows(cand, tv_scr, idx_ref, payload=cidx)
    tv = tv_scr[...]
    e = jnp.exp(tv - tv[0:1])
    g_ref[...] = e / jnp.sum(e, axis=0, keepdims=True)


def _route(x, mod, wpq, keys, *, seq, tb):
    T, D = x.shape
    kd = 2 * PEER_HALF
    return pl.pallas_call(
        _route_kernel,
        grid=(T // tb, PEER_HEADS),
        in_specs=[
            pl.BlockSpec((tb, D), lambda i, h: (i, 0)),
            pl.BlockSpec((1, SUBLANES, D), lambda i, h: ((i * tb) // seq, 0, 0)),
            pl.BlockSpec((D, kd), lambda i, h: (0, h)),
            pl.BlockSpec((1, 2, N_KEYS, PEER_HALF), lambda i, h: (h, 0, 0, 0)),
        ],
        out_specs=[
            pl.BlockSpec((tb, D), lambda i, h: (i, 0)),
            pl.BlockSpec((PEER_TOPK, tb), lambda i, h: (h, i)),
            pl.BlockSpec((PEER_TOPK, tb), lambda i, h: (h, i)),
        ],
        out_shape=[
            jax.ShapeDtypeStruct((T, D), F32),
            jax.ShapeDtypeStruct((N_PAIRS, T), jnp.int32),
            jax.ShapeDtypeStruct((N_PAIRS, T), F32),
        ],
        scratch_shapes=[
            pltpu.VMEM((tb, D), BF16),
            pltpu.VMEM((PEER_TOPK, tb), F32), pltpu.VMEM((PEER_TOPK, tb), F32),
            pltpu.VMEM((PEER_TOPK, tb), F32), pltpu.VMEM((PEER_TOPK, tb), F32),
            pltpu.VMEM((PEER_TOPK, tb), F32),
        ],
        compiler_params=_cparams(("arbitrary", "arbitrary")),
        name="route",
    )(x, mod, wpq, keys)


def _load_table(tab_hbm, tab_vmem, sem):
    @pl.when(pl.program_id(0) == 0)
    def _():
        cp = pltpu.make_async_copy(tab_hbm, tab_vmem, sem)
        cp.start()
        cp.wait()


TILE_ROWS = 2 * SUBLANES
HALF_PAIRS = N_PAIRS // 2
W_ROWS = HALF_PAIRS * TILE_ROWS
LOW_HALF_ROW = 0


def _pair_tiles(tab_vmem, rows_ref, t):
    rows_t = rows_ref.at[t]
    tiles = [tab_vmem[rows_t[k]] for k in range(N_PAIRS)]
    wa = pltpu.bitcast(jnp.concatenate(tiles[:HALF_PAIRS], axis=0), BF16)
    wb = pltpu.bitcast(jnp.concatenate(tiles[HALF_PAIRS:], axis=0), BF16)
    return jnp.concatenate([wa, wb], axis=1)


TOKENS_PER_TRIP = 8


def _token_loop(tb, token):
    def trip(i, carry):
        for u in range(TOKENS_PER_TRIP):
            token(i * TOKENS_PER_TRIP + u, carry)
        return carry
    lax.fori_loop(0, tb // TOKENS_PER_TRIP, trip, 0)


def _row_parity(half_bit):
    return jnp.where(half_bit == 1, LOW_HALF_ROW, 1 - LOW_HALF_ROW)


def _expansion_constants():
    col = np.arange(2 * W_ROWS)
    blk, j, r = col // W_ROWS, (col % W_ROWS) // TILE_ROWS, col % TILE_ROWS
    pair = blk * HALF_PAIRS + j
    g = np.zeros((2 * W_ROWS, 2 * N_PAIRS), np.float32)
    g[col, (r % 2) * N_PAIRS + pair] = 1.0
    e = np.zeros((N_PAIRS, 2 * W_ROWS), np.float32)
    e[pair, col] = 1.0
    return g, e


def _peer_u_kernel(rows_ref, h_ref, idx_ref, g_ref, gsum_ref, spread_ref, tab_hbm, cexp_ref,
                   tab_vmem, stage, sem, *, tb):
    _load_table(tab_hbm, tab_vmem, sem)
    sub = lax.broadcasted_iota(jnp.int32, (TILE_ROWS, W_ROWS), 0)
    lane = lax.broadcasted_iota(jnp.int32, (TILE_ROWS, W_ROWS), 1)
    diag = ((lane >> 1) & (SUBLANES - 1)) == (sub & (SUBLANES - 1))
    nt = (((1,), (1,)), ((), ()))

    def token(t, carry):
        w = _pair_tiles(tab_vmem, rows_ref, t)
        xt = h_ref[t].astype(BF16)
        z = jnp.zeros_like(xt)
        x2 = jnp.concatenate([jnp.concatenate([xt, z], axis=1),
                              jnp.concatenate([z, xt], axis=1)], axis=0)
        res = lax.dot_general(x2, w, nt, preferred_element_type=F32)
        m = jnp.where(diag, res, 0.0)
        stage[pl.ds(t, 1), 0:W_ROWS] = jnp.sum(m[:SUBLANES], axis=0, keepdims=True)
        stage[pl.ds(t, 1), W_ROWS:] = jnp.sum(m[SUBLANES:], axis=0, keepdims=True)
        return carry

    _token_loop(tb, token)
    s = stage[...]
    hi = s.astype(BF16)
    lo = (s - hi.astype(F32)).astype(BF16)
    r = (jnp.dot(hi, gsum_ref[...], preferred_element_type=F32)
         + jnp.dot(lo, gsum_ref[...], preferred_element_type=F32))
    half_bit = idx_ref[...] >> 13
    a = jnp.where(_row_parity(half_bit) == 0, r[:, :N_PAIRS], r[:, N_PAIRS:])
    coef = (g_ref[...] * jax.nn.gelu(a)).astype(BF16)
    cexp = jnp.dot(coef, spread_ref[...], preferred_element_type=F32)
    pexp = jnp.dot(_row_parity(half_bit).astype(BF16), spread_ref[...], preferred_element_type=F32)
    col_par = (lax.broadcasted_iota(jnp.int32, cexp.shape, 1) & 1).astype(F32)
    cexp_ref[...] = jnp.where(pexp == col_par, cexp, 0.0)


def _peer_u(rows_tm, h3, idx_tm, g_tm, gsum, spread, tab, *, tb):
    T = h3.shape[0]
    kern = functools.partial(_peer_u_kernel, tb=tb)
    full = lambda a: pl.BlockSpec(a.shape, lambda i: (0,) * a.ndim)
    return pl.pallas_call(
        kern,
        grid=(T // tb,),
        in_specs=[
            pl.BlockSpec((tb, N_PAIRS), lambda i: (i, 0), memory_space=pltpu.SMEM),
            pl.BlockSpec((tb, SUBLANES, LANES), lambda i: (i, 0, 0)),
            pl.BlockSpec((tb, N_PAIRS), lambda i: (i, 0)),
            pl.BlockSpec((tb, N_PAIRS), lambda i: (i, 0)),
            full(gsum), full(spread),
            pl.BlockSpec(memory_space=pl.ANY),
        ],
        out_specs=pl.BlockSpec((tb, 2 * W_ROWS), lambda i: (i, 0)),
        out_shape=jax.ShapeDtypeStruct((T, 2 * W_ROWS), F32),
        scratch_shapes=[
            pltpu.VMEM((HALF_EXPERTS, SUBLANES, LANES), jnp.uint32),
            pltpu.VMEM((tb, 2 * W_ROWS), F32),
            pltpu.SemaphoreType.DMA,
        ],
        compiler_params=_cparams(("arbitrary",), TABLE_VMEM_LIMIT),
        name="peer_u",
    )(rows_tm, h3, idx_tm, g_tm, gsum, spread, tab)


def _peer_v_kernel(rows_ref, cexp_ref, x_ref, g2_ref, tab_hbm, xo_ref, tab_vmem, sem, *, tb):
    _load_table(tab_hbm, tab_vmem, sem)
    g2 = g2_ref[0]
    sub = lax.broadcasted_iota(jnp.int32, (SUBLANES, LANES), 0)
    lane = lax.broadcasted_iota(jnp.int32, (SUBLANES, LANES), 1)
    diag = ((lane >> 1) & (SUBLANES - 1)) == sub

    def token(t, carry):
        w = _pair_tiles(tab_vmem, rows_ref, t)
        slab = cexp_ref[t]
        blocks = []
        for blk in range(2):
            chunks = [jnp.where(diag, jnp.broadcast_to(slab[blk * SUBLANES + q:blk * SUBLANES + q + 1], diag.shape), 0.0)
                      for q in range(SUBLANES)]
            blocks.append(jnp.concatenate(chunks, axis=1))
        c = jnp.concatenate(blocks, axis=0).astype(BF16)
        res = jnp.dot(c, w, preferred_element_type=F32)
        out = res[:SUBLANES, :LANES] + res[SUBLANES:, LANES:]
        xo_ref[t] = x_ref[t] + g2 * out
        return carry

    _token_loop(tb, token)


def _peer_v(rows_tm, cexp3, x3, g2, tab, *, seq, tb):
    T = x3.shape[0]
    kern = functools.partial(_peer_v_kernel, tb=tb)
    tok = pl.BlockSpec((tb, SUBLANES, LANES), lambda i: (i, 0, 0))
    return pl.pallas_call(
        kern,
        grid=(T // tb,),
        in_specs=[
            pl.BlockSpec((tb, N_PAIRS), lambda i: (i, 0), memory_space=pltpu.SMEM),
            pl.BlockSpec((tb, TILE_ROWS, LANES), lambda i: (i, 0, 0)),
            tok,
            pl.BlockSpec((1, SUBLANES, LANES), lambda i: ((i * tb) // seq, 0, 0)),
            pl.BlockSpec(memory_space=pl.ANY),
        ],
        out_specs=tok,
        out_shape=jax.ShapeDtypeStruct(x3.shape, F32),
        scratch_shapes=[
            pltpu.VMEM((HALF_EXPERTS, SUBLANES, LANES), jnp.uint32),
            pltpu.SemaphoreType.DMA,
        ],
        compiler_params=_cparams(("arbitrary",), TABLE_VMEM_LIMIT),
        name="peer_v",
    )(rows_tm, cexp3, x3, g2, tab)


def _final_kernel(x_ref, g_ref, o_ref):
    o_ref[...] = _rms(x_ref[...]) * g_ref[...]


def _final_norm(x, gain, tb):
    T, D = x.shape
    return pl.pallas_call(
        _final_kernel,
        grid=(T // tb,),
        in_specs=[pl.BlockSpec((tb, D), lambda i: (i, 0)), pl.BlockSpec((1, D), lambda i: (0, 0))],
        out_specs=pl.BlockSpec((tb, D), lambda i: (i, 0)),
        out_shape=jax.ShapeDtypeStruct((T, D), F32),
        compiler_params=_cparams(("arbitrary",)),
        name="final_norm",
    )(x, gain)


def _pack_table(tab):
    bits = lax.bitcast_convert_type(tab.astype(BF16), jnp.uint16).astype(jnp.uint32)
    packed = (bits[:HALF_EXPERTS] << 16) | bits[HALF_EXPERTS:]
    return packed.reshape(HALF_EXPERTS, SUBLANES, LANES)


def _pad_heads(w, per_head, offset):
    k = w.shape[0]
    w = w.reshape(k, MLA_HEADS, per_head)
    out = jnp.zeros((k, MLA_HEADS, HEAD_PAD), w.dtype)
    out = out.at[:, :, offset:offset + per_head].set(w)
    return out.reshape(k, MLA_HEADS * HEAD_PAD)


def _block_size(seq, want):
    tb = min(want, seq)
    assert seq % tb == 0
    return tb


def _layer(xf, mod_l, cos, sin, w_in, w_s, b_s, q_norm, w_uq, kv_norm, w_ukv, w_pool, pool_scale,
           w_a, w_b, w_c, w_o, w_pq, sub_keys, u_tab, v_tab, *, batch, seq):
    T, D = xf.shape
    tb_in = _block_size(seq, 512)
    tb_mg = _block_size(seq, 256)
    tb_rt = _block_size(seq, 256)
    ta = _block_size(seq, 1024)
    tb_peer = LANES

    sp = (0, 256, 512, 896, 1152, 1184, 1440)
    kr_pad = jnp.zeros((D, LANES), F32).at[:, QK_NOPE:QK_NOPE + QK_ROPE].set(w_in[:, sp[4]:sp[5]])
    w1 = jnp.concatenate([w_in[:, :sp[4]], w_in[:, sp[5]:sp[6]], kr_pad], axis=1).astype(BF16)
    wg = w_in[:, sp[6]:].astype(BF16)
    bs_full = jnp.repeat(b_s.T, A_GROUP_DIM, axis=1)
    wuq = _pad_heads(w_uq, QK_DIM, 0).astype(BF16)
    wukv = w_ukv.reshape(KV_LORA, MLA_HEADS, QK_NOPE + V_DIM)
    wk = _pad_heads(wukv[:, :, :QK_NOPE].reshape(KV_LORA, -1), QK_NOPE, 0).astype(BF16)
    wv = _pad_heads(wukv[:, :, QK_NOPE:].reshape(KV_LORA, -1), V_DIM, 0).astype(BF16)
    wpool = jax.scipy.linalg.block_diag(*[w_pool[g] for g in range(len(POOL_WINDOWS))]).astype(BF16)
    wb = w_b.reshape(MLA_HEADS, V_DIM, D)
    wb = jnp.pad(wb, ((0, 0), (0, HEAD_PAD - V_DIM), (0, 0))).reshape(MLA_HEADS * HEAD_PAD, D).astype(BF16)

    ya, yc, q, k, v = _inproj(
        xf, mod_l, w1, w_s, bs_full, q_norm.reshape(1, -1), wuq, kv_norm.reshape(1, -1),
        wk, wv, cos, sin, wpool, pool_scale.reshape(1, -1), seq=seq, tb=tb_in)
    yb = _attention(q, k.T, v, batch=batch, seq=seq, ta=ta)
    x1 = _merge(xf, mod_l, ya, yb, yc, wg, w_a.astype(BF16), wb, w_c.astype(BF16),
                w_o.astype(BF16), seq=seq, tb=tb_mg)
    h2, idx_t, g_t = _route(x1, mod_l, w_pq.astype(BF16), sub_keys.astype(BF16), seq=seq, tb=tb_rt)
    idx_tm = idx_t.T
    rows_tm = idx_tm & (HALF_EXPERTS - 1)
    gsum, spread = (jnp.asarray(a, BF16) for a in _expansion_constants())
    cexp = _peer_u(rows_tm, h2.reshape(T, SUBLANES, LANES), idx_tm, g_t.T, gsum, spread,
                   _pack_table(u_tab), tb=tb_peer)
    g2 = mod_l[:, 5].reshape(batch, SUBLANES, LANES)
    x2 = _peer_v(rows_tm, cexp.reshape(T, TILE_ROWS, LANES), x1.reshape(T, SUBLANES, LANES), g2,
                 _pack_table(v_tab), seq=seq, tb=tb_peer)
    return x2.reshape(T, D), (x1, h2)


def kernel(x, c, positions, w_mod, b_mod, w_in, w_s, b_s, q_norm, w_uq, kv_norm, w_ukv, w_pool, pool_scale, w_a, w_b, w_c, w_o, w_pq, sub_keys, u_tab, v_tab, final_norm):
    B, S, D = x.shape
    L = w_mod.shape[0]
    T = B * S
    assert D == D_MODEL and S % A_CHUNK == 0

    c_pad = jnp.zeros((SUBLANES, D), F32).at[:B].set(c)
    mod = _modulation(c_pad, w_mod, b_mod)[:, :B].reshape(L, B, 6, D)
    mod = jnp.pad(mod, ((0, 0), (0, 0), (0, SUBLANES - 6), (0, 0)))

    inv_freq = 1.0 / (ROPE_THETA ** (jnp.arange(0, QK_ROPE, 2, dtype=F32) / QK_ROPE))
    half = QK_ROPE // 2
    freq_lane = jnp.zeros((LANES,), F32)
    freq_lane = freq_lane.at[QK_NOPE:QK_NOPE + half].set(inv_freq)
    freq_lane = freq_lane.at[QK_NOPE + half:QK_NOPE + QK_ROPE].set(inv_freq)
    ang = positions.astype(F32).reshape(T, 1) * freq_lane[None, :]
    cos, sin = _rope_tables(ang, _block_size(T, 1024))

    xf = x.reshape(T, D)
    for l in range(L):
        xf, _ = _layer(xf, mod[l], cos, sin, w_in[l], w_s[l], b_s[l], q_norm[l], w_uq[l], kv_norm[l],
                       w_ukv[l], w_pool[l], pool_scale[l], w_a[l], w_b[l], w_c[l], w_o[l], w_pq[l],
                       sub_keys[l], u_tab[l], v_tab[l], batch=B, seq=S)

    out = _final_norm(xf, final_norm.reshape(1, D), _block_size(T, 1024))
    return out.reshape(B, S, D)
```

```python
import functools
import math

import numpy as np
import jax
import jax.numpy as jnp
from jax import lax
from jax.experimental import pallas as pl
from jax.experimental.pallas import tpu as pltpu

F32 = jnp.float32
BF16 = jnp.bfloat16

EPS = 1e-6
D_MODEL = 1024
A_WIDTH = 256
A_GROUPS = 4
A_GROUP_DIM = A_WIDTH // A_GROUPS
A_CHUNK = 128
MLA_HEADS = 8
Q_LORA = 384
KV_LORA = 256
QK_NOPE = 64
QK_ROPE = 32
V_DIM = 64
QK_DIM = QK_NOPE + QK_ROPE
ROPE_THETA = 10000.0
POOL_WINDOWS = (2, 4, 8, 16)
C_WIDTH = 256
C_GROUP = C_WIDTH // len(POOL_WINDOWS)
PEER_HEADS = 8
N_KEYS = 128
N_EXPERTS = N_KEYS * N_KEYS
PEER_HALF = 128
PEER_TOPK = 16
N_PAIRS = PEER_HEADS * PEER_TOPK

LANES = 128
SUBLANES = 8
HEAD_PAD = LANES
HALO = 16
NEG = -1e30
ATT_CHUNKS = 8
ONES_LANE = V_DIM

VMEM_LIMIT = 48 * 1024 * 1024
TABLE_VMEM_LIMIT = 56 * 1024 * 1024

_C_ZU, _C_ZV, _C_CQ, _C_CKV, _C_ZP, _C_KR, _C_END = 0, 256, 512, 896, 1152, 1408, 1536


def _rms(x):
    return x * lax.rsqrt(jnp.mean(x * x, axis=-1, keepdims=True) + EPS)


def _cparams(sem, limit=VMEM_LIMIT):
    return pltpu.CompilerParams(dimension_semantics=sem, vmem_limit_bytes=limit)


def _mod_kernel(c_ref, w_ref, b_ref, o_ref):
    c = c_ref[...]
    ca = c * jax.nn.sigmoid(c)
    o_ref[0] = jnp.dot(ca, w_ref[0], preferred_element_type=F32) + b_ref[0]


def _modulation(c_pad, w_mod, b_mod):
    L, D, six_d = w_mod.shape
    nb = six_d // D
    rows = c_pad.shape[0]
    return pl.pallas_call(
        _mod_kernel,
        grid=(L, nb),
        in_specs=[
            pl.BlockSpec((rows, D), lambda l, j: (0, 0)),
            pl.BlockSpec((1, D, D), lambda l, j: (l, 0, j)),
            pl.BlockSpec((1, 1, D), lambda l, j: (l, 0, j)),
        ],
        out_specs=pl.BlockSpec((1, rows, D), lambda l, j: (l, 0, j)),
        out_shape=jax.ShapeDtypeStruct((L, rows, six_d), F32),
        compiler_params=_cparams(("arbitrary", "arbitrary")),
        name="modulation",
    )(c_pad, w_mod, b_mod.reshape(L, 1, six_d))


def _rope_kernel(ang_ref, cos_ref, sin_ref):
    ang = ang_ref[...]
    lane = lax.broadcasted_iota(jnp.int32, ang.shape, 1)
    first_half = lane < QK_NOPE + QK_ROPE // 2
    cos_ref[...] = jnp.cos(ang)
    s = jnp.sin(ang)
    sin_ref[...] = jnp.where(first_half, -s, s)


def _rope_tables(ang, tb):
    T = ang.shape[0]
    spec = pl.BlockSpec((tb, LANES), lambda i: (i, 0))
    return pl.pallas_call(
        _rope_kernel,
        grid=(T // tb,),
        in_specs=[spec],
        out_specs=[spec, spec],
        out_shape=[jax.ShapeDtypeStruct((T, LANES), F32)] * 2,
        compiler_params=_cparams(("arbitrary",)),
        name="rope_tables",
    )(ang)


def _rope_apply(xh, cos, sin):
    lane = lax.broadcasted_iota(jnp.int32, xh.shape, 1)
    half = QK_ROPE // 2
    partner = jnp.where(lane < QK_NOPE + half,
                        pltpu.roll(xh, LANES - half, axis=1),
                        pltpu.roll(xh, half, axis=1))
    return xh * cos + partner * sin


def _inproj_kernel(x_ref, xh_ref, mod_ref, w1_ref, ws_ref, bs_ref, qn_ref, wuq_ref, kvn_ref,
                   wk_ref, wv_ref, cos_ref, sin_ref, wpool_ref, pscale_ref,
                   ya_ref, yc_ref, q_ref, k_ref, v_ref, *, seq, tb):
    i = pl.program_id(0)
    m = mod_ref[0]
    sh1, sc1 = m[0:1], m[1:2]
    h = (_rms(x_ref[...]) * (1.0 + sc1) + sh1).astype(BF16)
    z = jnp.dot(h, w1_ref[...], preferred_element_type=F32)

    gu = jax.nn.gelu(z[:, _C_ZU:_C_ZV])
    gv = jax.nn.gelu(z[:, _C_ZV:_C_CQ])
    mu = jnp.mean(gv, axis=-1, keepdims=True)
    dv = gv - mu
    vn = (dv * lax.rsqrt(jnp.mean(dv * dv, axis=-1, keepdims=True) + EPS)).astype(BF16)
    r = lax.broadcasted_iota(jnp.int32, (A_CHUNK, A_CHUNK), 0)
    cc = lax.broadcasted_iota(jnp.int32, (A_CHUNK, A_CHUNK), 1)
    lane_a = lax.broadcasted_iota(jnp.int32, (A_CHUNK, A_WIDTH), 1)
    wtril = [jnp.where(r >= cc, ws_ref[g], 0.0).astype(BF16) for g in range(A_GROUPS)]
    for ci in range(tb // A_CHUNK):
        lo = ci * A_CHUNK
        vc = vn[lo:lo + A_CHUNK]
        mixed = bs_ref[...]
        for g in range(A_GROUPS):
            dg = jnp.dot(wtril[g], vc, preferred_element_type=F32)
            in_g = (lane_a >= g * A_GROUP_DIM) & (lane_a < (g + 1) * A_GROUP_DIM)
            mixed = mixed + jnp.where(in_g, dg, 0.0)
        ya_ref[lo:lo + A_CHUNK, :] = (gu[lo:lo + A_CHUNK] * mixed).astype(BF16)

    zp = z[:, _C_ZP:_C_KR]
    t0 = (i * tb) % seq
    hh = (_rms(xh_ref[...]) * (1.0 + sc1) + sh1).astype(BF16)
    zph = jnp.dot(hh, w1_ref[:, _C_ZP:_C_KR], preferred_element_type=F32)
    zph = zph * jnp.where(t0 == 0, 0.0, 1.0)
    zext = jnp.concatenate([zph, zp], axis=0)
    n_ext = HALO + tb
    s2 = zext + pltpu.roll(zext, 1, axis=0)
    s4 = s2 + pltpu.roll(s2, 2, axis=0)
    s8 = s4 + pltpu.roll(s4, 4, axis=0)
    s16 = s8 + pltpu.roll(s8, 8, axis=0)
    lane_c = lax.broadcasted_iota(jnp.int32, (n_ext, C_WIDTH), 1)
    grp = lane_c // C_GROUP
    win = jnp.where(grp == 0, s2, jnp.where(grp == 1, s4, jnp.where(grp == 2, s8, s16)))
    wsize = jnp.where(grp == 0, 2, jnp.where(grp == 1, 4, jnp.where(grp == 2, 8, 16)))
    tpos = t0 - HALO + lax.broadcasted_iota(jnp.int32, (n_ext, C_WIDTH), 0)
    cnt = jnp.minimum(tpos + 1, wsize).astype(F32)
    pooled = (win / jnp.maximum(cnt, 1.0) - zext)[HALO:]
    yc = jnp.dot(pooled.astype(BF16), wpool_ref[...], preferred_element_type=F32)
    yc_ref[...] = (yc * pscale_ref[...]).astype(BF16)

    cos = cos_ref[...]
    sin = sin_ref[...]
    cqn = (_rms(z[:, _C_CQ:_C_CKV]) * qn_ref[...]).astype(BF16)
    q = jnp.dot(cqn, wuq_ref[...], preferred_element_type=F32)
    ckvn = (_rms(z[:, _C_CKV:_C_ZP]) * kvn_ref[...]).astype(BF16)
    kn = jnp.dot(ckvn, wk_ref[...], preferred_element_type=F32)
    vv = jnp.dot(ckvn, wv_ref[...], preferred_element_type=F32)
    lane_v = lax.broadcasted_iota(jnp.int32, vv.shape, 1)
    v_ref[...] = jnp.where(lane_v % HEAD_PAD == ONES_LANE, 1.0, vv).astype(BF16)
    krope = _rope_apply(z[:, _C_KR:_C_END], cos, sin)
    qscale = math.log2(math.e) / math.sqrt(QK_DIM)
    for hd in range(MLA_HEADS):
        sl = slice(hd * HEAD_PAD, (hd + 1) * HEAD_PAD)
        q_ref[:, sl] = (_rope_apply(q[:, sl], cos, sin) * qscale).astype(BF16)
        k_ref[:, sl] = (kn[:, sl] + krope).astype(BF16)


def _inproj(x, mod, w1, w_s, bs_full, q_norm, wuq, kv_norm, wk, wv, cos, sin, wpool, pscale, *, seq, tb):
    T, D = x.shape
    hpb = tb // HALO
    kern = functools.partial(_inproj_kernel, seq=seq, tb=tb)
    full = lambda a: pl.BlockSpec(a.shape, lambda i: (0,) * a.ndim)
    hw = MLA_HEADS * HEAD_PAD
    return pl.pallas_call(
        kern,
        grid=(T // tb,),
        in_specs=[
            pl.BlockSpec((tb, D), lambda i: (i, 0)),
            pl.BlockSpec((HALO, D), lambda i: (jnp.maximum(i * hpb - 1, 0), 0)),
            pl.BlockSpec((1, SUBLANES, D), lambda i: ((i * tb) // seq, 0, 0)),
            full(w1), full(w_s), full(bs_full), full(q_norm), full(wuq), full(kv_norm),
            full(wk), full(wv),
            pl.BlockSpec((tb, LANES), lambda i: (i, 0)),
            pl.BlockSpec((tb, LANES), lambda i: (i, 0)),
            full(wpool), full(pscale),
        ],
        out_specs=[
            pl.BlockSpec((tb, A_WIDTH), lambda i: (i, 0)),
            pl.BlockSpec((tb, C_WIDTH), lambda i: (i, 0)),
            pl.BlockSpec((tb, hw), lambda i: (i, 0)),
            pl.BlockSpec((tb, hw), lambda i: (i, 0)),
            pl.BlockSpec((tb, hw), lambda i: (i, 0)),
        ],
        out_shape=[
            jax.ShapeDtypeStruct((T, A_WIDTH), BF16),
            jax.ShapeDtypeStruct((T, C_WIDTH), BF16),
            jax.ShapeDtypeStruct((T, hw), BF16),
            jax.ShapeDtypeStruct((T, hw), BF16),
            jax.ShapeDtypeStruct((T, hw), BF16),
        ],
        compiler_params=_cparams(("arbitrary",)),
        name="inproj",
    )(x, x, mod, w1, w_s, bs_full, q_norm, wuq, kv_norm, wk, wv, cos, sin, wpool, pscale)


def _attn_kernel(q_ref, kt_ref, v_ref, o_ref, m_scr, acc_scr, *, ta):
    qi = pl.program_id(2)
    ki = pl.program_id(3)

    @pl.when(ki == 0)
    def _():
        m_scr[...] = jnp.full(m_scr.shape, NEG, F32)
        acc_scr[...] = jnp.zeros(acc_scr.shape, F32)

    def block(diagonal):
        n_chunks = ATT_CHUNKS if ta % (ATT_CHUNKS * SUBLANES) == 0 else 1
        rows = ta // n_chunks
        sls = [slice(h * rows, (h + 1) * rows) for h in range(n_chunks)]
        m_prevs = [m_scr[sl, :] for sl in sls]
        acc_prevs = [acc_scr[sl, :] for sl in sls]

        def scores(h):
            s = jnp.dot(q_ref[sls[h], :], kt_ref[...], preferred_element_type=F32)
            if diagonal:
                row = h * rows + lax.broadcasted_iota(jnp.int32, s.shape, 0)
                col = lax.broadcasted_iota(jnp.int32, s.shape, 1)
                s = jnp.where(col <= row, s, NEG)
            return s

        def softmax(h, s):
            m_new = jnp.maximum(m_prevs[h], jnp.max(s, axis=1, keepdims=True))
            alpha = jnp.exp2(m_prevs[h] - m_new)
            return m_new, alpha, jnp.exp2(s - m_new[:, 0:1]).astype(BF16)

        def weighted(h, alpha, p):
            return alpha * acc_prevs[h] + jnp.dot(p, v_ref[...], preferred_element_type=F32)

        s_next = scores(0)
        pending = None
        done = []
        for h in range(n_chunks + 1):
            if h < n_chunks:
                s_cur = s_next
                if h + 1 < n_chunks:
                    s_next = scores(h + 1)
                m_new, alpha, p = softmax(h, s_cur)
            if pending is not None:
                hp, m_p, alpha_p, p_p = pending
                done.append((hp, m_p, weighted(hp, alpha_p, p_p)))
            pending = (h, m_new, alpha, p)
        for hp, m_p, acc in done:
            m_scr[sls[hp], :] = m_p
            acc_scr[sls[hp], :] = acc

    @pl.when(ki < qi)
    def _():
        block(False)

    @pl.when(ki == qi)
    def _():
        block(True)
        acc = acc_scr[...]
        o_ref[...] = (acc / acc[:, ONES_LANE:ONES_LANE + 1]).astype(BF16)


def _attention(q, kt, v, *, batch, seq, ta):
    T, hw = q.shape
    nq = seq // ta
    kern = functools.partial(_attn_kernel, ta=ta)
    qspec = pl.BlockSpec((ta, HEAD_PAD), lambda b, h, qi, ki: (b * nq + qi, h))
    kspec = pl.BlockSpec((ta, HEAD_PAD), lambda b, h, qi, ki: (b * nq + jnp.minimum(ki, qi), h))
    ktspec = pl.BlockSpec((HEAD_PAD, ta), lambda b, h, qi, ki: (h, b * nq + jnp.minimum(ki, qi)))
    return pl.pallas_call(
        kern,
        grid=(batch, MLA_HEADS, nq, nq),
        in_specs=[qspec, ktspec, kspec],
        out_specs=qspec,
        out_shape=jax.ShapeDtypeStruct((T, hw), BF16),
        scratch_shapes=[pltpu.VMEM((ta, HEAD_PAD), F32)] * 2,
        compiler_params=_cparams(("arbitrary",) * 4),
        name="attention",
    )(q, kt, v)


def _merge_kernel(x_ref, mod_ref, ya_ref, yb_ref, yc_ref, wg_ref, wa_ref, wb_ref, wc_ref, wo_ref,
                  xo_ref):
    D = D_MODEL
    x = x_ref[...]
    m = mod_ref[0]
    sh1, sc1, g1 = m[0:1], m[1:2], m[2:3]
    h = (_rms(x) * (1.0 + sc1) + sh1).astype(BF16)
    gates = jax.nn.sigmoid(jnp.dot(h, wg_ref[...], preferred_element_type=F32))
    merged = gates[:, 0:D] * jnp.dot(ya_ref[...], wa_ref[...], preferred_element_type=F32)
    merged += gates[:, D:2 * D] * jnp.dot(yb_ref[...], wb_ref[...], preferred_element_type=F32)
    merged += gates[:, 2 * D:3 * D] * jnp.dot(yc_ref[...], wc_ref[...], preferred_element_type=F32)
    xo_ref[...] = x + g1 * jnp.dot(merged.astype(BF16), wo_ref[...], preferred_element_type=F32)


def _merge(x, mod, ya, yb, yc, wg, wa, wb, wc, wo, *, seq, tb):
    T, D = x.shape
    full = lambda a: pl.BlockSpec(a.shape, lambda i: (0,) * a.ndim)
    row = lambda w: pl.BlockSpec((tb, w), lambda i: (i, 0))
    return pl.pallas_call(
        _merge_kernel,
        grid=(T // tb,),
        in_specs=[row(D), pl.BlockSpec((1, SUBLANES, D), lambda i: ((i * tb) // seq, 0, 0)),
                  row(ya.shape[1]), row(yb.shape[1]), row(yc.shape[1]),
                  full(wg), full(wa), full(wb), full(wc), full(wo)],
        out_specs=row(D),
        out_shape=jax.ShapeDtypeStruct((T, D), F32),
        compiler_params=_cparams(("arbitrary",)),
        name="merge",
    )(x, mod, ya, yb, yc, wg, wa, wb, wc, wo)


def _top16_rows(s, vals_ref, idx_ref, payload=None):
    _top16_lockstep([(s, vals_ref, idx_ref, payload)])


def _top16_lockstep(problems):
    state = [p[0] for p in problems]
    rows = lax.broadcasted_iota(jnp.int32, state[0].shape, 0).astype(F32)
    for j in range(PEER_TOPK):
        for i, (_, vals_ref, idx_ref, payload) in enumerate(problems):
            s = state[i]
            mx = jnp.max(s, axis=0, keepdims=True)
            pos = jnp.min(jnp.where(s == mx, rows, float(s.shape[0])), axis=0, keepdims=True)
            hit = rows == pos
            vals_ref[j:j + 1, :] = mx
            if payload is None:
                idx_ref[j:j + 1, :] = pos
            else:
                picked = jnp.sum(jnp.where(hit, payload, 0.0), axis=0, keepdims=True)
                idx_ref[j:j + 1, :] = picked.astype(jnp.int32)
            state[i] = jnp.where(hit, NEG, s)


def _route_kernel(x_ref, mod_ref, wpq_ref, keys_ref, h2_ref, idx_ref, g_ref,
                  h2_scr, v1_scr, i1_scr, v2_scr, i2_scr, tv_scr):
    hd = pl.program_id(1)

    @pl.when(hd == 0)
    def _():
        m = mod_ref[0]
        sh2, sc2 = m[3:4], m[4:5]
        h2 = _rms(x_ref[...]) * (1.0 + sc2) + sh2
        h2_ref[...] = h2
        h2_scr[...] = h2.astype(BF16)

    q = jnp.dot(h2_scr[...], wpq_ref[...], preferred_element_type=F32).astype(BF16)
    nt = (((1,), (1,)), ((), ()))
    s1 = lax.dot_general(keys_ref[0, 0], q[:, :PEER_HALF], nt, preferred_element_type=F32)
    s2 = lax.dot_general(keys_ref[0, 1], q[:, PEER_HALF:], nt, preferred_element_type=F32)
    _top16_lockstep([(s1, v1_scr, i1_scr, None), (s2, v2_scr, i2_scr, None)])
    v1, v2 = v1_scr[...], v2_scr[...]
    i1, i2 = i1_scr[...] * N_KEYS, i2_scr[...]

    r8 = lax.broadcasted_iota(jnp.int32, (SUBLANES, v1.shape[1]), 0)
    low = r8 < 4

    def two(a0, a1, nb0, nb1, val, idx):
        va = jnp.where(low, val[a0:a0 + 1], val[a1:a1 + 1])
        ia = jnp.where(low, idx[a0:a0 + 1], idx[a1:a1 + 1])
        ok = r8 < jnp.where(low, nb0, nb1 + 4)
        return va, ia, ok

    v2lo, i2lo = v2[0:SUBLANES], i2[0:SUBLANES]
    v2rep = jnp.where(low, v2lo, pltpu.roll(v2lo, 4, axis=0))
    i2rep = jnp.where(low, i2lo, pltpu.roll(i2lo, 4, axis=0))
    cand = [v1[0:1] + v2[0:SUBLANES], v1[0:1] + v2[SUBLANES:], v1[1:2] + v2lo]
    cidx = [i1[0:1] + i2[0:SUBLANES], i1[0:1] + i2[SUBLANES:], i1[1:2] + i2lo]
    for a, nb in ((2, 5), (3, 4)):
        cand.append(jnp.where(r8 < nb, v1[a:a + 1] + v2lo, NEG))
        cidx.append(i1[a:a + 1] + i2lo)
    for a0, a1, nb0, nb1 in ((4, 5, 3, 2), (6, 7, 2, 2)):
        va, ia, ok = two(a0, a1, nb0, nb1, v1, i1)
        cand.append(jnp.where(ok, va + v2rep, NEG))
        cidx.append(ia + i2rep)
    cand.append(v1[SUBLANES:] + v2[0:1])
    cidx.append(i1[SUBLANES:] + i2[0:1])
    cand = jnp.concatenate(cand, axis=0)
    cidx = jnp.concatenate(cidx, axis=0)
    _top16_rows(cand, tv_scr, idx_ref, payload=cidx)
    tv = tv_scr[...]
    e = jnp.exp(tv - tv[0:1])
    g_ref[...] = e / jnp.sum(e, axis=0, keepdims=True)


def _route(x, mod, wpq, keys, *, seq, tb):
    T, D = x.shape
    kd = 2 * PEER_HALF
    return pl.pallas_call(
        _route_kernel,
        grid=(T // tb, PEER_HEADS),
        in_specs=[
            pl.BlockSpec((tb, D), lambda i, h: (i, 0)),
            pl.BlockSpec((1, SUBLANES, D), lambda i, h: ((i * tb) // seq, 0, 0)),
            pl.BlockSpec((D, kd), lambda i, h: (0, h)),
            pl.BlockSpec((1, 2, N_KEYS, PEER_HALF), lambda i, h: (h, 0, 0, 0)),
        ],
        out_specs=[
            pl.BlockSpec((tb, D), lambda i, h: (i, 0)),
            pl.BlockSpec((PEER_TOPK, tb), lambda i, h: (h, i)),
            pl.BlockSpec((PEER_TOPK, tb), lambda i, h: (h, i)),
        ],
        out_shape=[
            jax.ShapeDtypeStruct((T, D), F32),
            jax.ShapeDtypeStruct((N_PAIRS, T), jnp.int32),
            jax.ShapeDtypeStruct((N_PAIRS, T), F32),
        ],
        scratch_shapes=[
            pltpu.VMEM((tb, D), BF16),
            pltpu.VMEM((PEER_TOPK, tb), F32), pltpu.VMEM((PEER_TOPK, tb), F32),
            pltpu.VMEM((PEER_TOPK, tb), F32), pltpu.VMEM((PEER_TOPK, tb), F32),
            pltpu.VMEM((PEER_TOPK, tb), F32),
        ],
        compiler_params=_cparams(("arbitrary", "arbitrary")),
        name="route",
    )(x, mod, wpq, keys)


def _load_table(tab_hbm, tab_vmem, sem):
    @pl.when(pl.program_id(0) == 0)
    def _():
        cp = pltpu.make_async_copy(tab_hbm, tab_vmem, sem)
        cp.start()
        cp.wait()


HALF_TILE = SUBLANES // 2
HALF_PAIRS = N_PAIRS // 2
W_ROWS = HALF_PAIRS * SUBLANES


def _pair_tiles(tab_vmem, rows_ref, t):
    rows_t = rows_ref.at[t]
    tiles = [tab_vmem[pl.ds(pl.multiple_of(rows_t[k], HALF_TILE), HALF_TILE), :] for k in range(N_PAIRS)]
    wa = pltpu.bitcast(jnp.concatenate(tiles[:HALF_PAIRS], axis=0), BF16)
    wb = pltpu.bitcast(jnp.concatenate(tiles[HALF_PAIRS:], axis=0), BF16)
    return jnp.concatenate([wa, wb], axis=1)


TOKENS_PER_TRIP = 8


def _token_loop(tb, token):
    def trip(i, carry):
        for u in range(TOKENS_PER_TRIP):
            token(i * TOKENS_PER_TRIP + u, carry)
        return carry
    lax.fori_loop(0, tb // TOKENS_PER_TRIP, trip, 0)


def _expansion_constants():
    col = np.arange(2 * W_ROWS)
    pair = (col // W_ROWS) * HALF_PAIRS + (col % W_ROWS) // SUBLANES
    g = np.zeros((2 * W_ROWS, N_PAIRS), np.float32)
    g[col, pair] = 1.0
    return g, g.T.copy()


def _peer_u_kernel(rows_ref, h_ref, g_ref, gsum_ref, spread_ref, tab_hbm, cexp_ref,
                   tab_vmem, stage, sem, *, tb):
    _load_table(tab_hbm, tab_vmem, sem)
    sub = lax.broadcasted_iota(jnp.int32, (2 * SUBLANES, W_ROWS), 0)
    lane = lax.broadcasted_iota(jnp.int32, (2 * SUBLANES, W_ROWS), 1)
    diag = (lane & (SUBLANES - 1)) == (sub & (SUBLANES - 1))
    nt = (((1,), (1,)), ((), ()))

    def token(t, carry):
        w = _pair_tiles(tab_vmem, rows_ref, t)
        xt = h_ref[t].astype(BF16)
        z = jnp.zeros_like(xt)
        x2 = jnp.concatenate([jnp.concatenate([xt, z], axis=1),
                              jnp.concatenate([z, xt], axis=1)], axis=0)
        res = lax.dot_general(x2, w, nt, preferred_element_type=F32)
        m = jnp.where(diag, res, 0.0)
        stage[pl.ds(t, 1), 0:W_ROWS] = jnp.sum(m[:SUBLANES], axis=0, keepdims=True)
        stage[pl.ds(t, 1), W_ROWS:] = jnp.sum(m[SUBLANES:], axis=0, keepdims=True)
        return carry

    _token_loop(tb, token)
    s = stage[...]
    hi = s.astype(BF16)
    lo = (s - hi.astype(F32)).astype(BF16)
    a = (jnp.dot(hi, gsum_ref[...], preferred_element_type=F32)
         + jnp.dot(lo, gsum_ref[...], preferred_element_type=F32))
    coef = (g_ref[...] * jax.nn.gelu(a)).astype(BF16)
    cexp_ref[...] = jnp.dot(coef, spread_ref[...], preferred_element_type=F32)


def _peer_u(rows_tm, h3, g_tm, gsum, spread, tab, *, tb):
    T = h3.shape[0]
    kern = functools.partial(_peer_u_kernel, tb=tb)
    full = lambda a: pl.BlockSpec(a.shape, lambda i: (0,) * a.ndim)
    return pl.pallas_call(
        kern,
        grid=(T // tb,),
        in_specs=[
            pl.BlockSpec((tb, N_PAIRS), lambda i: (i, 0), memory_space=pltpu.SMEM),
            pl.BlockSpec((tb, SUBLANES, LANES), lambda i: (i, 0, 0)),
            pl.BlockSpec((tb, N_PAIRS), lambda i: (i, 0)),
            full(gsum), full(spread),
            pl.BlockSpec(memory_space=pl.ANY),
        ],
        out_specs=pl.BlockSpec((tb, 2 * W_ROWS), lambda i: (i, 0)),
        out_shape=jax.ShapeDtypeStruct((T, 2 * W_ROWS), F32),
        scratch_shapes=[
            pltpu.VMEM((N_EXPERTS * HALF_TILE, LANES), jnp.uint32),
            pltpu.VMEM((tb, 2 * W_ROWS), F32),
            pltpu.SemaphoreType.DMA,
        ],
        compiler_params=_cparams(("arbitrary",), TABLE_VMEM_LIMIT),
        name="peer_u",
    )(rows_tm, h3, g_tm, gsum, spread, tab)


def _peer_v_kernel(rows_ref, cexp_ref, x_ref, g2_ref, tab_hbm, xo_ref, tab_vmem, sem, *, tb):
    _load_table(tab_hbm, tab_vmem, sem)
    g2 = g2_ref[0]
    sub = lax.broadcasted_iota(jnp.int32, (SUBLANES, LANES), 0)
    lane = lax.broadcasted_iota(jnp.int32, (SUBLANES, LANES), 1)
    diag = (lane & (SUBLANES - 1)) == sub
    n_chunks = W_ROWS // LANES

    def token(t, carry):
        w = _pair_tiles(tab_vmem, rows_ref, t)
        slab = cexp_ref[t]
        blocks = []
        for blk in range(2):
            chunks = [jnp.where(diag, jnp.broadcast_to(slab[blk * n_chunks + q:blk * n_chunks + q + 1], diag.shape), 0.0)
                      for q in range(n_chunks)]
            blocks.append(jnp.concatenate(chunks, axis=1))
        c = jnp.concatenate(blocks, axis=0).astype(BF16)
        res = jnp.dot(c, w, preferred_element_type=F32)
        out = res[:SUBLANES, :LANES] + res[SUBLANES:, LANES:]
        xo_ref[t] = x_ref[t] + g2 * out
        return carry

    _token_loop(tb, token)


def _peer_v(rows_tm, cexp3, x3, g2, tab, *, seq, tb):
    T = x3.shape[0]
    kern = functools.partial(_peer_v_kernel, tb=tb)
    tok = pl.BlockSpec((tb, SUBLANES, LANES), lambda i: (i, 0, 0))
    return pl.pallas_call(
        kern,
        grid=(T // tb,),
        in_specs=[
            pl.BlockSpec((tb, N_PAIRS), lambda i: (i, 0), memory_space=pltpu.SMEM),
            tok,
            tok,
            pl.BlockSpec((1, SUBLANES, LANES), lambda i: ((i * tb) // seq, 0, 0)),
            pl.BlockSpec(memory_space=pl.ANY),
        ],
        out_specs=tok,
        out_shape=jax.ShapeDtypeStruct(x3.shape, F32),
        scratch_shapes=[
            pltpu.VMEM((N_EXPERTS * HALF_TILE, LANES), jnp.uint32),
            pltpu.SemaphoreType.DMA,
        ],
        compiler_params=_cparams(("arbitrary",), TABLE_VMEM_LIMIT),
        name="peer_v",
    )(rows_tm, cexp3, x3, g2, tab)


def _final_kernel(x_ref, g_ref, o_ref):
    o_ref[...] = _rms(x_ref[...]) * g_ref[...]


def _final_norm(x, gain, tb):
    T, D = x.shape
    return pl.pallas_call(
        _final_kernel,
        grid=(T // tb,),
        in_specs=[pl.BlockSpec((tb, D), lambda i: (i, 0)), pl.BlockSpec((1, D), lambda i: (0, 0))],
        out_specs=pl.BlockSpec((tb, D), lambda i: (i, 0)),
        out_shape=jax.ShapeDtypeStruct((T, D), F32),
        compiler_params=_cparams(("arbitrary",)),
        name="final_norm",
    )(x, gain)


def _pack_table(tab):
    bits = lax.bitcast_convert_type(tab.astype(BF16), jnp.uint16).astype(jnp.uint32)
    bits = bits.reshape(N_EXPERTS, HALF_TILE, 2, LANES)
    packed = (bits[:, :, 1] << 16) | bits[:, :, 0]
    return packed.reshape(N_EXPERTS * HALF_TILE, LANES)


def _pad_heads(w, per_head, offset):
    k = w.shape[0]
    w = w.reshape(k, MLA_HEADS, per_head)
    out = jnp.zeros((k, MLA_HEADS, HEAD_PAD), w.dtype)
    out = out.at[:, :, offset:offset + per_head].set(w)
    return out.reshape(k, MLA_HEADS * HEAD_PAD)


def _block_size(seq, want):
    tb = min(want, seq)
    assert seq % tb == 0
    return tb


def _layer(xf, mod_l, cos, sin, w_in, w_s, b_s, q_norm, w_uq, kv_norm, w_ukv, w_pool, pool_scale,
           w_a, w_b, w_c, w_o, w_pq, sub_keys, u_tab, v_tab, *, batch, seq):
    T, D = xf.shape
    tb_in = _block_size(seq, 512)
    tb_mg = _block_size(seq, 256)
    tb_rt = _block_size(seq, 256)
    ta = _block_size(seq, 2048)
    tb_peer = LANES

    sp = (0, 256, 512, 896, 1152, 1184, 1440)
    kr_pad = jnp.zeros((D, LANES), F32).at[:, QK_NOPE:QK_NOPE + QK_ROPE].set(w_in[:, sp[4]:sp[5]])
    w1 = jnp.concatenate([w_in[:, :sp[4]], w_in[:, sp[5]:sp[6]], kr_pad], axis=1).astype(BF16)
    wg = w_in[:, sp[6]:].astype(BF16)
    bs_full = jnp.repeat(b_s.T, A_GROUP_DIM, axis=1)
    wuq = _pad_heads(w_uq, QK_DIM, 0).astype(BF16)
    wukv = w_ukv.reshape(KV_LORA, MLA_HEADS, QK_NOPE + V_DIM)
    wk = _pad_heads(wukv[:, :, :QK_NOPE].reshape(KV_LORA, -1), QK_NOPE, 0).astype(BF16)
    wv = _pad_heads(wukv[:, :, QK_NOPE:].reshape(KV_LORA, -1), V_DIM, 0).astype(BF16)
    wpool = jax.scipy.linalg.block_diag(*[w_pool[g] for g in range(len(POOL_WINDOWS))]).astype(BF16)
    wb = w_b.reshape(MLA_HEADS, V_DIM, D)
    wb = jnp.pad(wb, ((0, 0), (0, HEAD_PAD - V_DIM), (0, 0))).reshape(MLA_HEADS * HEAD_PAD, D).astype(BF16)

    ya, yc, q, k, v = _inproj(
        xf, mod_l, w1, w_s, bs_full, q_norm.reshape(1, -1), wuq, kv_norm.reshape(1, -1),
        wk, wv, cos, sin, wpool, pool_scale.reshape(1, -1), seq=seq, tb=tb_in)
    yb = _attention(q, k.T, v, batch=batch, seq=seq, ta=ta)
    x1 = _merge(xf, mod_l, ya, yb, yc, wg, w_a.astype(BF16), wb, w_c.astype(BF16),
                w_o.astype(BF16), seq=seq, tb=tb_mg)
    h2, idx_t, g_t = _route(x1, mod_l, w_pq.astype(BF16), sub_keys.astype(BF16), seq=seq, tb=tb_rt)
    rows_tm = idx_t.T * HALF_TILE
    gsum, spread = (jnp.asarray(a, BF16) for a in _expansion_constants())
    cexp = _peer_u(rows_tm, h2.reshape(T, SUBLANES, LANES), g_t.T, gsum, spread,
                   _pack_table(u_tab), tb=tb_peer)
    g2 = mod_l[:, 5].reshape(batch, SUBLANES, LANES)
    x2 = _peer_v(rows_tm, cexp.reshape(T, SUBLANES, LANES), x1.reshape(T, SUBLANES, LANES), g2,
                 _pack_table(v_tab), seq=seq, tb=tb_peer)
    return x2.reshape(T, D), (x1, h2)


def kernel(x, c, positions, w_mod, b_mod, w_in, w_s, b_s, q_norm, w_uq, kv_norm, w_ukv, w_pool, pool_scale, w_a, w_b, w_c, w_o, w_pq, sub_keys, u_tab, v_tab, final_norm):
    B, S, D = x.shape
    L = w_mod.shape[0]
    T = B * S
    assert D == D_MODEL and S % A_CHUNK == 0

    c_pad = jnp.zeros((SUBLANES, D), F32).at[:B].set(c)
    mod = _modulation(c_pad, w_mod, b_mod)[:, :B].reshape(L, B, 6, D)
    mod = jnp.pad(mod, ((0, 0), (0, 0), (0, SUBLANES - 6), (0, 0)))

    inv_freq = 1.0 / (ROPE_THETA ** (jnp.arange(0, QK_ROPE, 2, dtype=F32) / QK_ROPE))
    half = QK_ROPE // 2
    freq_lane = jnp.zeros((LANES,), F32)
    freq_lane = freq_lane.at[QK_NOPE:QK_NOPE + half].set(inv_freq)
    freq_lane = freq_lane.at[QK_NOPE + half:QK_NOPE + QK_ROPE].set(inv_freq)
    ang = positions.astype(F32).reshape(T, 1) * freq_lane[None, :]
    cos, sin = _rope_tables(ang, _block_size(T, 1024))

    xf = x.reshape(T, D)
    for l in range(L):
        xf, _ = _layer(xf, mod[l], cos, sin, w_in[l], w_s[l], b_s[l], q_norm[l], w_uq[l], kv_norm[l],
                       w_ukv[l], w_pool[l], pool_scale[l], w_a[l], w_b[l], w_c[l], w_o[l], w_pq[l],
                       sub_keys[l], u_tab[l], v_tab[l], batch=B, seq=S)

    out = _final_norm(xf, final_norm.reshape(1, D), _block_size(T, 1024))
    return out.reshape(B, S, D)
```

```python
import functools
import math

import numpy as np
import jax
import jax.numpy as jnp
from jax import lax
from jax.experimental import pallas as pl
from jax.experimental.pallas import tpu as pltpu

F32 = jnp.float32
BF16 = jnp.bfloat16

EPS = 1e-6
D_MODEL = 1024
A_WIDTH = 256
A_GROUPS = 4
A_GROUP_DIM = A_WIDTH // A_GROUPS
A_CHUNK = 128
MLA_HEADS = 8
Q_LORA = 384
KV_LORA = 256
QK_NOPE = 64
QK_ROPE = 32
V_DIM = 64
QK_DIM = QK_NOPE + QK_ROPE
ROPE_THETA = 10000.0
POOL_WINDOWS = (2, 4, 8, 16)
C_WIDTH = 256
C_GROUP = C_WIDTH // len(POOL_WINDOWS)
PEER_HEADS = 8
N_KEYS = 128
N_EXPERTS = N_KEYS * N_KEYS
PEER_HALF = 128
PEER_TOPK = 16
N_PAIRS = PEER_HEADS * PEER_TOPK

LANES = 128
SUBLANES = 8
HEAD_PAD = LANES
HALO = 16
NEG = -1e30
ATT_CHUNKS = 8
ONES_LANE = V_DIM

VMEM_LIMIT = 48 * 1024 * 1024
TABLE_VMEM_LIMIT = 56 * 1024 * 1024

_C_ZU, _C_ZV, _C_CQ, _C_CKV, _C_ZP, _C_KR, _C_END = 0, 256, 512, 896, 1152, 1408, 1536


def _rms(x):
    return x * lax.rsqrt(jnp.mean(x * x, axis=-1, keepdims=True) + EPS)


def _cparams(sem, limit=VMEM_LIMIT):
    return pltpu.CompilerParams(dimension_semantics=sem, vmem_limit_bytes=limit)


def _mod_kernel(c_ref, w_ref, b_ref, o_ref):
    c = c_ref[...]
    ca = c * jax.nn.sigmoid(c)
    o_ref[0] = jnp.dot(ca, w_ref[0], preferred_element_type=F32) + b_ref[0]


def _modulation(c_pad, w_mod, b_mod):
    L, D, six_d = w_mod.shape
    nb = six_d // D
    rows = c_pad.shape[0]
    return pl.pallas_call(
        _mod_kernel,
        grid=(L, nb),
        in_specs=[
            pl.BlockSpec((rows, D), lambda l, j: (0, 0)),
            pl.BlockSpec((1, D, D), lambda l, j: (l, 0, j)),
            pl.BlockSpec((1, 1, D), lambda l, j: (l, 0, j)),
        ],
        out_specs=pl.BlockSpec((1, rows, D), lambda l, j: (l, 0, j)),
        out_shape=jax.ShapeDtypeStruct((L, rows, six_d), F32),
        compiler_params=_cparams(("arbitrary", "arbitrary")),
        name="modulation",
    )(c_pad, w_mod, b_mod.reshape(L, 1, six_d))


def _rope_kernel(ang_ref, cos_ref, sin_ref):
    ang = ang_ref[...]
    lane = lax.broadcasted_iota(jnp.int32, ang.shape, 1)
    first_half = lane < QK_NOPE + QK_ROPE // 2
    cos_ref[...] = jnp.cos(ang)
    s = jnp.sin(ang)
    sin_ref[...] = jnp.where(first_half, -s, s)


def _rope_tables(ang, tb):
    T = ang.shape[0]
    spec = pl.BlockSpec((tb, LANES), lambda i: (i, 0))
    return pl.pallas_call(
        _rope_kernel,
        grid=(T // tb,),
        in_specs=[spec],
        out_specs=[spec, spec],
        out_shape=[jax.ShapeDtypeStruct((T, LANES), F32)] * 2,
        compiler_params=_cparams(("arbitrary",)),
        name="rope_tables",
    )(ang)


def _rope_apply(xh, cos, sin):
    lane = lax.broadcasted_iota(jnp.int32, xh.shape, 1)
    half = QK_ROPE // 2
    partner = jnp.where(lane < QK_NOPE + half,
                        pltpu.roll(xh, LANES - half, axis=1),
                        pltpu.roll(xh, half, axis=1))
    return xh * cos + partner * sin


def _inproj_kernel(x_ref, xh_ref, mod_ref, w1_ref, ws_ref, bs_ref, qn_ref, wuq_ref, kvn_ref,
                   wk_ref, wv_ref, cos_ref, sin_ref, wpool_ref, pscale_ref,
                   ya_ref, yc_ref, q_ref, k_ref, v_ref, *, seq, tb):
    i = pl.program_id(0)
    m = mod_ref[0]
    sh1, sc1 = m[0:1], m[1:2]
    h = (_rms(x_ref[...]) * (1.0 + sc1) + sh1).astype(BF16)
    z = jnp.dot(h, w1_ref[...], preferred_element_type=F32)

    gu = jax.nn.gelu(z[:, _C_ZU:_C_ZV])
    gv = jax.nn.gelu(z[:, _C_ZV:_C_CQ])
    mu = jnp.mean(gv, axis=-1, keepdims=True)
    dv = gv - mu
    vn = (dv * lax.rsqrt(jnp.mean(dv * dv, axis=-1, keepdims=True) + EPS)).astype(BF16)
    r = lax.broadcasted_iota(jnp.int32, (A_CHUNK, A_CHUNK), 0)
    cc = lax.broadcasted_iota(jnp.int32, (A_CHUNK, A_CHUNK), 1)
    lane_a = lax.broadcasted_iota(jnp.int32, (A_CHUNK, A_WIDTH), 1)
    wtril = [jnp.where(r >= cc, ws_ref[g], 0.0).astype(BF16) for g in range(A_GROUPS)]
    for ci in range(tb // A_CHUNK):
        lo = ci * A_CHUNK
        vc = vn[lo:lo + A_CHUNK]
        mixed = bs_ref[...]
        for g in range(A_GROUPS):
            dg = jnp.dot(wtril[g], vc, preferred_element_type=F32)
            in_g = (lane_a >= g * A_GROUP_DIM) & (lane_a < (g + 1) * A_GROUP_DIM)
            mixed = mixed + jnp.where(in_g, dg, 0.0)
        ya_ref[lo:lo + A_CHUNK, :] = (gu[lo:lo + A_CHUNK] * mixed).astype(BF16)

    zp = z[:, _C_ZP:_C_KR]
    t0 = (i * tb) % seq
    hh = (_rms(xh_ref[...]) * (1.0 + sc1) + sh1).astype(BF16)
    zph = jnp.dot(hh, w1_ref[:, _C_ZP:_C_KR], preferred_element_type=F32)
    zph = zph * jnp.where(t0 == 0, 0.0, 1.0)
    zext = jnp.concatenate([zph, zp], axis=0)
    n_ext = HALO + tb
    s2 = zext + pltpu.roll(zext, 1, axis=0)
    s4 = s2 + pltpu.roll(s2, 2, axis=0)
    s8 = s4 + pltpu.roll(s4, 4, axis=0)
    s16 = s8 + pltpu.roll(s8, 8, axis=0)
    lane_c = lax.broadcasted_iota(jnp.int32, (n_ext, C_WIDTH), 1)
    grp = lane_c // C_GROUP
    win = jnp.where(grp == 0, s2, jnp.where(grp == 1, s4, jnp.where(grp == 2, s8, s16)))
    wsize = jnp.where(grp == 0, 2, jnp.where(grp == 1, 4, jnp.where(grp == 2, 8, 16)))
    tpos = t0 - HALO + lax.broadcasted_iota(jnp.int32, (n_ext, C_WIDTH), 0)
    cnt = jnp.minimum(tpos + 1, wsize).astype(F32)
    pooled = (win / jnp.maximum(cnt, 1.0) - zext)[HALO:]
    yc = jnp.dot(pooled.astype(BF16), wpool_ref[...], preferred_element_type=F32)
    yc_ref[...] = (yc * pscale_ref[...]).astype(BF16)

    cos = cos_ref[...]
    sin = sin_ref[...]
    cqn = (_rms(z[:, _C_CQ:_C_CKV]) * qn_ref[...]).astype(BF16)
    q = jnp.dot(cqn, wuq_ref[...], preferred_element_type=F32)
    ckvn = (_rms(z[:, _C_CKV:_C_ZP]) * kvn_ref[...]).astype(BF16)
    kn = jnp.dot(ckvn, wk_ref[...], preferred_element_type=F32)
    vv = jnp.dot(ckvn, wv_ref[...], preferred_element_type=F32)
    lane_v = lax.broadcasted_iota(jnp.int32, vv.shape, 1)
    v_ref[...] = jnp.where(lane_v % HEAD_PAD == ONES_LANE, 1.0, vv).astype(BF16)
    krope = _rope_apply(z[:, _C_KR:_C_END], cos, sin)
    qscale = math.log2(math.e) / math.sqrt(QK_DIM)
    for hd in range(MLA_HEADS):
        sl = slice(hd * HEAD_PAD, (hd + 1) * HEAD_PAD)
        q_ref[:, sl] = (_rope_apply(q[:, sl], cos, sin) * qscale).astype(BF16)
        k_ref[:, sl] = (kn[:, sl] + krope).astype(BF16)


def _inproj(x, mod, w1, w_s, bs_full, q_norm, wuq, kv_norm, wk, wv, cos, sin, wpool, pscale, *, seq, tb):
    T, D = x.shape
    hpb = tb // HALO
    kern = functools.partial(_inproj_kernel, seq=seq, tb=tb)
    full = lambda a: pl.BlockSpec(a.shape, lambda i: (0,) * a.ndim)
    hw = MLA_HEADS * HEAD_PAD
    return pl.pallas_call(
        kern,
        grid=(T // tb,),
        in_specs=[
            pl.BlockSpec((tb, D), lambda i: (i, 0)),
            pl.BlockSpec((HALO, D), lambda i: (jnp.maximum(i * hpb - 1, 0), 0)),
            pl.BlockSpec((1, SUBLANES, D), lambda i: ((i * tb) // seq, 0, 0)),
            full(w1), full(w_s), full(bs_full), full(q_norm), full(wuq), full(kv_norm),
            full(wk), full(wv),
            pl.BlockSpec((tb, LANES), lambda i: (i, 0)),
            pl.BlockSpec((tb, LANES), lambda i: (i, 0)),
            full(wpool), full(pscale),
        ],
        out_specs=[
            pl.BlockSpec((tb, A_WIDTH), lambda i: (i, 0)),
            pl.BlockSpec((tb, C_WIDTH), lambda i: (i, 0)),
            pl.BlockSpec((tb, hw), lambda i: (i, 0)),
            pl.BlockSpec((tb, hw), lambda i: (i, 0)),
            pl.BlockSpec((tb, hw), lambda i: (i, 0)),
        ],
        out_shape=[
            jax.ShapeDtypeStruct((T, A_WIDTH), BF16),
            jax.ShapeDtypeStruct((T, C_WIDTH), BF16),
            jax.ShapeDtypeStruct((T, hw), BF16),
            jax.ShapeDtypeStruct((T, hw), BF16),
            jax.ShapeDtypeStruct((T, hw), BF16),
        ],
        compiler_params=_cparams(("arbitrary",)),
        name="inproj",
    )(x, x, mod, w1, w_s, bs_full, q_norm, wuq, kv_norm, wk, wv, cos, sin, wpool, pscale)


def _attn_kernel(q_ref, kt_ref, v_ref, o_ref, m_scr, acc_scr, *, ta):
    qi = pl.program_id(2)
    ki = pl.program_id(3)

    @pl.when(ki == 0)
    def _():
        m_scr[...] = jnp.full(m_scr.shape, NEG, F32)
        acc_scr[...] = jnp.zeros(acc_scr.shape, F32)

    def block(diagonal):
        n_chunks = ATT_CHUNKS if ta % (ATT_CHUNKS * SUBLANES) == 0 else 1
        rows = ta // n_chunks
        sls = [slice(h * rows, (h + 1) * rows) for h in range(n_chunks)]
        m_prevs = [m_scr[sl, :] for sl in sls]
        acc_prevs = [acc_scr[sl, :] for sl in sls]

        def n_cols(h):
            return (h + 1) * rows if diagonal else ta

        def scores(h):
            s = jnp.dot(q_ref[sls[h], :], kt_ref[:, :n_cols(h)], preferred_element_type=F32)
            if diagonal:
                row = h * rows + lax.broadcasted_iota(jnp.int32, s.shape, 0)
                col = lax.broadcasted_iota(jnp.int32, s.shape, 1)
                s = jnp.where(col <= row, s, NEG)
            return s

        def softmax(h, s):
            m_new = jnp.maximum(m_prevs[h], jnp.max(s, axis=1, keepdims=True))
            alpha = jnp.exp2(m_prevs[h] - m_new)
            return m_new, alpha, jnp.exp2(s - m_new[:, 0:1]).astype(BF16)

        def weighted(h, alpha, p):
            return alpha * acc_prevs[h] + jnp.dot(p, v_ref[:n_cols(h), :], preferred_element_type=F32)

        s_next = scores(0)
        pending = None
        done = []
        for h in range(n_chunks + 1):
            if h < n_chunks:
                s_cur = s_next
                if h + 1 < n_chunks:
                    s_next = scores(h + 1)
                m_new, alpha, p = softmax(h, s_cur)
            if pending is not None:
                hp, m_p, alpha_p, p_p = pending
                done.append((hp, m_p, weighted(hp, alpha_p, p_p)))
            pending = (h, m_new, alpha, p)
        for hp, m_p, acc in done:
            m_scr[sls[hp], :] = m_p
            acc_scr[sls[hp], :] = acc

    @pl.when(ki < qi)
    def _():
        block(False)

    @pl.when(ki == qi)
    def _():
        block(True)
        acc = acc_scr[...]
        o_ref[...] = (acc / acc[:, ONES_LANE:ONES_LANE + 1]).astype(BF16)


def _attention(q, kt, v, *, batch, seq, ta):
    T, hw = q.shape
    nq = seq // ta
    kern = functools.partial(_attn_kernel, ta=ta)
    qspec = pl.BlockSpec((ta, HEAD_PAD), lambda b, h, qi, ki: (b * nq + qi, h))
    kspec = pl.BlockSpec((ta, HEAD_PAD), lambda b, h, qi, ki: (b * nq + jnp.minimum(ki, qi), h))
    ktspec = pl.BlockSpec((HEAD_PAD, ta), lambda b, h, qi, ki: (h, b * nq + jnp.minimum(ki, qi)))
    return pl.pallas_call(
        kern,
        grid=(batch, MLA_HEADS, nq, nq),
        in_specs=[qspec, ktspec, kspec],
        out_specs=qspec,
        out_shape=jax.ShapeDtypeStruct((T, hw), BF16),
        scratch_shapes=[pltpu.VMEM((ta, HEAD_PAD), F32)] * 2,
        compiler_params=_cparams(("arbitrary",) * 4),
        name="attention",
    )(q, kt, v)


def _merge_kernel(x_ref, mod_ref, ya_ref, yb_ref, yc_ref, wg_ref, wa_ref, wb_ref, wc_ref, wo_ref,
                  xo_ref):
    D = D_MODEL
    x = x_ref[...]
    m = mod_ref[0]
    sh1, sc1, g1 = m[0:1], m[1:2], m[2:3]
    h = (_rms(x) * (1.0 + sc1) + sh1).astype(BF16)
    gates = jax.nn.sigmoid(jnp.dot(h, wg_ref[...], preferred_element_type=F32))
    merged = gates[:, 0:D] * jnp.dot(ya_ref[...], wa_ref[...], preferred_element_type=F32)
    merged += gates[:, D:2 * D] * jnp.dot(yb_ref[...], wb_ref[...], preferred_element_type=F32)
    merged += gates[:, 2 * D:3 * D] * jnp.dot(yc_ref[...], wc_ref[...], preferred_element_type=F32)
    xo_ref[...] = x + g1 * jnp.dot(merged.astype(BF16), wo_ref[...], preferred_element_type=F32)


def _merge(x, mod, ya, yb, yc, wg, wa, wb, wc, wo, *, seq, tb):
    T, D = x.shape
    full = lambda a: pl.BlockSpec(a.shape, lambda i: (0,) * a.ndim)
    row = lambda w: pl.BlockSpec((tb, w), lambda i: (i, 0))
    return pl.pallas_call(
        _merge_kernel,
        grid=(T // tb,),
        in_specs=[row(D), pl.BlockSpec((1, SUBLANES, D), lambda i: ((i * tb) // seq, 0, 0)),
                  row(ya.shape[1]), row(yb.shape[1]), row(yc.shape[1]),
                  full(wg), full(wa), full(wb), full(wc), full(wo)],
        out_specs=row(D),
        out_shape=jax.ShapeDtypeStruct((T, D), F32),
        compiler_params=_cparams(("arbitrary",)),
        name="merge",
    )(x, mod, ya, yb, yc, wg, wa, wb, wc, wo)


def _top16_rows(s, vals_ref, idx_ref, payload=None):
    _top16_lockstep([(s, vals_ref, idx_ref, payload)])


def _top16_lockstep(problems):
    state = [p[0] for p in problems]
    rows = lax.broadcasted_iota(jnp.int32, state[0].shape, 0).astype(F32)
    for j in range(PEER_TOPK):
        for i, (_, vals_ref, idx_ref, payload) in enumerate(problems):
            s = state[i]
            mx = jnp.max(s, axis=0, keepdims=True)
            pos = jnp.min(jnp.where(s == mx, rows, float(s.shape[0])), axis=0, keepdims=True)
            hit = rows == pos
            vals_ref[j:j + 1, :] = mx
            if payload is None:
                idx_ref[j:j + 1, :] = pos
            else:
                picked = jnp.sum(jnp.where(hit, payload, 0.0), axis=0, keepdims=True)
                idx_ref[j:j + 1, :] = picked.astype(jnp.int32)
            state[i] = jnp.where(hit, NEG, s)


def _route_kernel(x_ref, mod_ref, wpq_ref, keys_ref, h2_ref, idx_ref, g_ref,
                  h2_scr, v1_scr, i1_scr, v2_scr, i2_scr, tv_scr):
    hd = pl.program_id(1)

    @pl.when(hd == 0)
    def _():
        m = mod_ref[0]
        sh2, sc2 = m[3:4], m[4:5]
        h2 = _rms(x_ref[...]) * (1.0 + sc2) + sh2
        h2_ref[...] = h2
        h2_scr[...] = h2.astype(BF16)

    q = jnp.dot(h2_scr[...], wpq_ref[...], preferred_element_type=F32).astype(BF16)
    nt = (((1,), (1,)), ((), ()))
    s1 = lax.dot_general(keys_ref[0, 0], q[:, :PEER_HALF], nt, preferred_element_type=F32)
    s2 = lax.dot_general(keys_ref[0, 1], q[:, PEER_HALF:], nt, preferred_element_type=F32)
    _top16_lockstep([(s1, v1_scr, i1_scr, None), (s2, v2_scr, i2_scr, None)])
    v1, v2 = v1_scr[...], v2_scr[...]
    i1, i2 = i1_scr[...] * N_KEYS, i2_scr[...]

    r8 = lax.broadcasted_iota(jnp.int32, (SUBLANES, v1.shape[1]), 0)
    low = r8 < 4

    def two(a0, a1, nb0, nb1, val, idx):
        va = jnp.where(low, val[a0:a0 + 1], val[a1:a1 + 1])
        ia = jnp.where(low, idx[a0:a0 + 1], idx[a1:a1 + 1])
        ok = r8 < jnp.where(low, nb0, nb1 + 4)
        return va, ia, ok

    v2lo, i2lo = v2[0:SUBLANES], i2[0:SUBLANES]
    v2rep = jnp.where(low, v2lo, pltpu.roll(v2lo, 4, axis=0))
    i2rep = jnp.where(low, i2lo, pltpu.roll(i2lo, 4, axis=0))
    cand = [v1[0:1] + v2[0:SUBLANES], v1[0:1] + v2[SUBLANES:], v1[1:2] + v2lo]
    cidx = [i1[0:1] + i2[0:SUBLANES], i1[0:1] + i2[SUBLANES:], i1[1:2] + i2lo]
    for a, nb in ((2, 5), (3, 4)):
        cand.append(jnp.where(r8 < nb, v1[a:a + 1] + v2lo, NEG))
        cidx.append(i1[a:a + 1] + i2lo)
    for a0, a1, nb0, nb1 in ((4, 5, 3, 2), (6, 7, 2, 2)):
        va, ia, ok = two(a0, a1, nb0, nb1, v1, i1)
        cand.append(jnp.where(ok, va + v2rep, NEG))
        cidx.append(ia + i2rep)
    cand.append(v1[SUBLANES:] + v2[0:1])
    cidx.append(i1[SUBLANES:] + i2[0:1])
    cand = jnp.concatenate(cand, axis=0)
    cidx = jnp.concatenate(cidx, axis=0)
    _top16_rows(cand, tv_scr, idx_ref, payload=cidx)
    tv = tv_scr[...]
    e = jnp.exp(tv - tv[0:1])
    g_ref[...] = e / jnp.sum(e, axis=0, keepdims=True)


def _route(x, mod, wpq, keys, *, seq, tb):
    T, D = x.shape
    kd = 2 * PEER_HALF
    return pl.pallas_call(
        _route_kernel,
        grid=(T // tb, PEER_HEADS),
        in_specs=[
            pl.BlockSpec((tb, D), lambda i, h: (i, 0)),
            pl.BlockSpec((1, SUBLANES, D), lambda i, h: ((i * tb) // seq, 0, 0)),
            pl.BlockSpec((D, kd), lambda i, h: (0, h)),
            pl.BlockSpec((1, 2, N_KEYS, PEER_HALF), lambda i, h: (h, 0, 0, 0)),
        ],
        out_specs=[
            pl.BlockSpec((tb, D), lambda i, h: (i, 0)),
            pl.BlockSpec((PEER_TOPK, tb), lambda i, h: (h, i)),
            pl.BlockSpec((PEER_TOPK, tb), lambda i, h: (h, i)),
        ],
        out_shape=[
            jax.ShapeDtypeStruct((T, D), F32),
            jax.ShapeDtypeStruct((N_PAIRS, T), jnp.int32),
            jax.ShapeDtypeStruct((N_PAIRS, T), F32),
        ],
        scratch_shapes=[
            pltpu.VMEM((tb, D), BF16),
            pltpu.VMEM((PEER_TOPK, tb), F32), pltpu.VMEM((PEER_TOPK, tb), F32),
            pltpu.VMEM((PEER_TOPK, tb), F32), pltpu.VMEM((PEER_TOPK, tb), F32),
            pltpu.VMEM((PEER_TOPK, tb), F32),
        ],
        compiler_params=_cparams(("arbitrary", "arbitrary")),
        name="route",
    )(x, mod, wpq, keys)


def _load_table(tab_hbm, tab_vmem, sem):
    @pl.when(pl.program_id(0) == 0)
    def _():
        cp = pltpu.make_async_copy(tab_hbm, tab_vmem, sem)
        cp.start()
        cp.wait()


HALF_TILE = SUBLANES // 2
HALF_PAIRS = N_PAIRS // 2
W_ROWS = HALF_PAIRS * SUBLANES


def _pair_tiles(tab_vmem, rows_ref, t):
    rows_t = rows_ref.at[t]
    tiles = [tab_vmem[pl.ds(pl.multiple_of(rows_t[k], HALF_TILE), HALF_TILE), :] for k in range(N_PAIRS)]
    wa = pltpu.bitcast(jnp.concatenate(tiles[:HALF_PAIRS], axis=0), BF16)
    wb = pltpu.bitcast(jnp.concatenate(tiles[HALF_PAIRS:], axis=0), BF16)
    return jnp.concatenate([wa, wb], axis=1)


TOKENS_PER_TRIP = 8


def _token_loop(tb, token):
    def trip(i, carry):
        for u in range(TOKENS_PER_TRIP):
            token(i * TOKENS_PER_TRIP + u, carry)
        return carry
    lax.fori_loop(0, tb // TOKENS_PER_TRIP, trip, 0)


def _expansion_constants():
    col = np.arange(2 * W_ROWS)
    pair = (col // W_ROWS) * HALF_PAIRS + (col % W_ROWS) // SUBLANES
    g = np.zeros((2 * W_ROWS, N_PAIRS), np.float32)
    g[col, pair] = 1.0
    return g, g.T.copy()


def _peer_u_kernel(rows_ref, h_ref, g_ref, gsum_ref, spread_ref, tab_hbm, cexp_ref,
                   tab_vmem, stage, sem, *, tb):
    _load_table(tab_hbm, tab_vmem, sem)
    sub = lax.broadcasted_iota(jnp.int32, (2 * SUBLANES, W_ROWS), 0)
    lane = lax.broadcasted_iota(jnp.int32, (2 * SUBLANES, W_ROWS), 1)
    diag = (lane & (SUBLANES - 1)) == (sub & (SUBLANES - 1))
    nt = (((1,), (1,)), ((), ()))

    def token(t, carry):
        w = _pair_tiles(tab_vmem, rows_ref, t)
        hrow = h_ref[pl.ds(t, 1), :]
        xt = jnp.concatenate([hrow[:, r * LANES:(r + 1) * LANES] for r in range(SUBLANES)], axis=0).astype(BF16)
        z = jnp.zeros_like(xt)
        x2 = jnp.concatenate([jnp.concatenate([xt, z], axis=1),
                              jnp.concatenate([z, xt], axis=1)], axis=0)
        res = lax.dot_general(x2, w, nt, preferred_element_type=F32)
        m = jnp.where(diag, res, 0.0)
        stage[pl.ds(t, 1), 0:W_ROWS] = jnp.sum(m[:SUBLANES], axis=0, keepdims=True)
        stage[pl.ds(t, 1), W_ROWS:] = jnp.sum(m[SUBLANES:], axis=0, keepdims=True)
        return carry

    _token_loop(tb, token)
    s = stage[...]
    hi = s.astype(BF16)
    lo = (s - hi.astype(F32)).astype(BF16)
    a = (jnp.dot(hi, gsum_ref[...], preferred_element_type=F32)
         + jnp.dot(lo, gsum_ref[...], preferred_element_type=F32))
    coef = (g_ref[...] * jax.nn.gelu(a)).astype(BF16)
    cexp_ref[...] = jnp.dot(coef, spread_ref[...], preferred_element_type=F32)


def _peer_u(rows_tm, h2, g_tm, gsum, spread, tab, *, tb):
    T = h2.shape[0]
    kern = functools.partial(_peer_u_kernel, tb=tb)
    full = lambda a: pl.BlockSpec(a.shape, lambda i: (0,) * a.ndim)
    return pl.pallas_call(
        kern,
        grid=(T // tb,),
        in_specs=[
            pl.BlockSpec((tb, N_PAIRS), lambda i: (i, 0), memory_space=pltpu.SMEM),
            pl.BlockSpec((tb, D_MODEL), lambda i: (i, 0)),
            pl.BlockSpec((tb, N_PAIRS), lambda i: (i, 0)),
            full(gsum), full(spread),
            pl.BlockSpec(memory_space=pl.ANY),
        ],
        out_specs=pl.BlockSpec((tb, 2 * W_ROWS), lambda i: (i, 0)),
        out_shape=jax.ShapeDtypeStruct((T, 2 * W_ROWS), F32),
        scratch_shapes=[
            pltpu.VMEM((N_EXPERTS * HALF_TILE, LANES), jnp.uint32),
            pltpu.VMEM((tb, 2 * W_ROWS), F32),
            pltpu.SemaphoreType.DMA,
        ],
        compiler_params=_cparams(("arbitrary",), TABLE_VMEM_LIMIT),
        name="peer_u",
    )(rows_tm, h2, g_tm, gsum, spread, tab)


def _peer_v_kernel(rows_ref, cexp_ref, x_ref, g2_ref, tab_hbm, xo_ref, tab_vmem, sem, *, tb):
    _load_table(tab_hbm, tab_vmem, sem)
    g2 = g2_ref[0]
    sub = lax.broadcasted_iota(jnp.int32, (SUBLANES, LANES), 0)
    lane = lax.broadcasted_iota(jnp.int32, (SUBLANES, LANES), 1)
    diag = (lane & (SUBLANES - 1)) == sub
    n_chunks = W_ROWS // LANES

    def token(t, carry):
        w = _pair_tiles(tab_vmem, rows_ref, t)
        crow = cexp_ref[pl.ds(t, 1), :]
        blocks = []
        for blk in range(2):
            lo = [(blk * n_chunks + q) * LANES for q in range(n_chunks)]
            chunks = [jnp.where(diag, jnp.broadcast_to(crow[:, c:c + LANES], diag.shape), 0.0) for c in lo]
            blocks.append(jnp.concatenate(chunks, axis=1))
        c = jnp.concatenate(blocks, axis=0).astype(BF16)
        res = jnp.dot(c, w, preferred_element_type=F32)
        out = g2 * (res[:SUBLANES, :LANES] + res[SUBLANES:, LANES:])
        delta = jnp.concatenate([out[r:r + 1, :] for r in range(SUBLANES)], axis=1)
        xo_ref[pl.ds(t, 1), :] = x_ref[pl.ds(t, 1), :] + delta
        return carry

    _token_loop(tb, token)


def _peer_v(rows_tm, cexp, x, g2, tab, *, seq, tb):
    T = x.shape[0]
    kern = functools.partial(_peer_v_kernel, tb=tb)
    tok = pl.BlockSpec((tb, D_MODEL), lambda i: (i, 0))
    return pl.pallas_call(
        kern,
        grid=(T // tb,),
        in_specs=[
            pl.BlockSpec((tb, N_PAIRS), lambda i: (i, 0), memory_space=pltpu.SMEM),
            pl.BlockSpec((tb, 2 * W_ROWS), lambda i: (i, 0)),
            tok,
            pl.BlockSpec((1, SUBLANES, LANES), lambda i: ((i * tb) // seq, 0, 0)),
            pl.BlockSpec(memory_space=pl.ANY),
        ],
        out_specs=tok,
        out_shape=jax.ShapeDtypeStruct(x.shape, F32),
        scratch_shapes=[
            pltpu.VMEM((N_EXPERTS * HALF_TILE, LANES), jnp.uint32),
            pltpu.SemaphoreType.DMA,
        ],
        compiler_params=_cparams(("arbitrary",), TABLE_VMEM_LIMIT),
        name="peer_v",
    )(rows_tm, cexp, x, g2, tab)


def _final_kernel(x_ref, g_ref, o_ref):
    o_ref[...] = _rms(x_ref[...]) * g_ref[...]


def _final_norm(x, gain, tb):
    T, D = x.shape
    return pl.pallas_call(
        _final_kernel,
        grid=(T // tb,),
        in_specs=[pl.BlockSpec((tb, D), lambda i: (i, 0)), pl.BlockSpec((1, D), lambda i: (0, 0))],
        out_specs=pl.BlockSpec((tb, D), lambda i: (i, 0)),
        out_shape=jax.ShapeDtypeStruct((T, D), F32),
        compiler_params=_cparams(("arbitrary",)),
        name="final_norm",
    )(x, gain)


def _pack_table(tab):
    lead = tab.shape[:-2]
    bits = lax.bitcast_convert_type(tab.astype(BF16), jnp.uint16).astype(jnp.uint32)
    bits = bits.reshape(*lead, N_EXPERTS, HALF_TILE, 2, LANES)
    packed = (bits[..., 1, :] << 16) | bits[..., 0, :]
    return packed.reshape(*lead, N_EXPERTS * HALF_TILE, LANES)


def _pad_heads(w, per_head, offset):
    k = w.shape[0]
    w = w.reshape(k, MLA_HEADS, per_head)
    out = jnp.zeros((k, MLA_HEADS, HEAD_PAD), w.dtype)
    out = out.at[:, :, offset:offset + per_head].set(w)
    return out.reshape(k, MLA_HEADS * HEAD_PAD)


def _block_size(seq, want):
    tb = min(want, seq)
    assert seq % tb == 0
    return tb


def _layer(xf, mod_l, cos, sin, w_in, w_s, b_s, q_norm, w_uq, kv_norm, w_ukv, w_pool, pool_scale,
           w_a, w_b, w_c, w_o, w_pq, sub_keys, u_packed, v_packed, *, batch, seq):
    T, D = xf.shape
    tb_in = _block_size(seq, 512)
    tb_mg = _block_size(seq, 256)
    tb_rt = _block_size(seq, 1024)
    ta = _block_size(seq, 2048)
    tb_peer = LANES

    sp = (0, 256, 512, 896, 1152, 1184, 1440)
    kr_pad = jnp.zeros((D, LANES), F32).at[:, QK_NOPE:QK_NOPE + QK_ROPE].set(w_in[:, sp[4]:sp[5]])
    w1 = jnp.concatenate([w_in[:, :sp[4]], w_in[:, sp[5]:sp[6]], kr_pad], axis=1).astype(BF16)
    wg = w_in[:, sp[6]:].astype(BF16)
    bs_full = jnp.repeat(b_s.T, A_GROUP_DIM, axis=1)
    wuq = _pad_heads(w_uq, QK_DIM, 0).astype(BF16)
    wukv = w_ukv.reshape(KV_LORA, MLA_HEADS, QK_NOPE + V_DIM)
    wk = _pad_heads(wukv[:, :, :QK_NOPE].reshape(KV_LORA, -1), QK_NOPE, 0).astype(BF16)
    wv = _pad_heads(wukv[:, :, QK_NOPE:].reshape(KV_LORA, -1), V_DIM, 0).astype(BF16)
    wpool = jax.scipy.linalg.block_diag(*[w_pool[g] for g in range(len(POOL_WINDOWS))]).astype(BF16)
    wb = w_b.reshape(MLA_HEADS, V_DIM, D)
    wb = jnp.pad(wb, ((0, 0), (0, HEAD_PAD - V_DIM), (0, 0))).reshape(MLA_HEADS * HEAD_PAD, D).astype(BF16)

    ya, yc, q, k, v = _inproj(
        xf, mod_l, w1, w_s, bs_full, q_norm.reshape(1, -1), wuq, kv_norm.reshape(1, -1),
        wk, wv, cos, sin, wpool, pool_scale.reshape(1, -1), seq=seq, tb=tb_in)
    yb = _attention(q, k.T, v, batch=batch, seq=seq, ta=ta)
    x1 = _merge(xf, mod_l, ya, yb, yc, wg, w_a.astype(BF16), wb, w_c.astype(BF16),
                w_o.astype(BF16), seq=seq, tb=tb_mg)
    h2, idx_t, g_t = _route(x1, mod_l, w_pq.astype(BF16), sub_keys.astype(BF16), seq=seq, tb=tb_rt)
    rows_tm = idx_t.T * HALF_TILE
    gsum, spread = (jnp.asarray(a, BF16) for a in _expansion_constants())
    cexp = _peer_u(rows_tm, h2, g_t.T, gsum, spread,
                   u_packed, tb=tb_peer)
    g2 = mod_l[:, 5].reshape(batch, SUBLANES, LANES)
    x2 = _peer_v(rows_tm, cexp, x1, g2, v_packed, seq=seq, tb=tb_peer)
    return x2, (x1, h2)


def kernel(x, c, positions, w_mod, b_mod, w_in, w_s, b_s, q_norm, w_uq, kv_norm, w_ukv, w_pool, pool_scale, w_a, w_b, w_c, w_o, w_pq, sub_keys, u_tab, v_tab, final_norm):
    B, S, D = x.shape
    L = w_mod.shape[0]
    T = B * S
    assert D == D_MODEL and S % A_CHUNK == 0

    c_pad = jnp.zeros((SUBLANES, D), F32).at[:B].set(c)
    mod = _modulation(c_pad, w_mod, b_mod)[:, :B].reshape(L, B, 6, D)
    mod = jnp.pad(mod, ((0, 0), (0, 0), (0, SUBLANES - 6), (0, 0)))

    inv_freq = 1.0 / (ROPE_THETA ** (jnp.arange(0, QK_ROPE, 2, dtype=F32) / QK_ROPE))
    half = QK_ROPE // 2
    freq_lane = jnp.zeros((LANES,), F32)
    freq_lane = freq_lane.at[QK_NOPE:QK_NOPE + half].set(inv_freq)
    freq_lane = freq_lane.at[QK_NOPE + half:QK_NOPE + QK_ROPE].set(inv_freq)
    ang = positions.astype(F32).reshape(T, 1) * freq_lane[None, :]
    cos, sin = _rope_tables(ang, _block_size(T, 1024))

    u_packed = _pack_table(u_tab)
    v_packed = _pack_table(v_tab)
    xf = x.reshape(T, D)
    for l in range(L):
        xf, _ = _layer(xf, mod[l], cos, sin, w_in[l], w_s[l], b_s[l], q_norm[l], w_uq[l], kv_norm[l],
                       w_ukv[l], w_pool[l], pool_scale[l], w_a[l], w_b[l], w_c[l], w_o[l], w_pq[l],
                       sub_keys[l], u_packed[l], v_packed[l], batch=B, seq=S)

    out = _final_norm(xf, final_norm.reshape(1, D), _block_size(T, 1024))
    return out.reshape(B, S, D)
```

```python
import functools
import math

import numpy as np
import jax
import jax.numpy as jnp
from jax import lax
from jax.experimental import pallas as pl
from jax.experimental.pallas import tpu as pltpu

F32 = jnp.float32
BF16 = jnp.bfloat16

EPS = 1e-6
D_MODEL = 1024
A_WIDTH = 256
A_GROUPS = 4
A_GROUP_DIM = A_WIDTH // A_GROUPS
A_CHUNK = 128
MLA_HEADS = 8
Q_LORA = 384
KV_LORA = 256
QK_NOPE = 64
QK_ROPE = 32
V_DIM = 64
QK_DIM = QK_NOPE + QK_ROPE
ROPE_THETA = 10000.0
POOL_WINDOWS = (2, 4, 8, 16)
C_WIDTH = 256
C_GROUP = C_WIDTH // len(POOL_WINDOWS)
PEER_HEADS = 8
N_KEYS = 128
N_EXPERTS = N_KEYS * N_KEYS
PEER_HALF = 128
PEER_TOPK = 16
N_PAIRS = PEER_HEADS * PEER_TOPK

LANES = 128
SUBLANES = 8
HEAD_PAD = LANES
HALO = 16
NEG = -1e30
ATT_CHUNKS = 8
ONES_LANE = V_DIM

VMEM_LIMIT = 48 * 1024 * 1024
TABLE_VMEM_LIMIT = 56 * 1024 * 1024

_C_ZU, _C_ZV, _C_CQ, _C_CKV, _C_ZP, _C_KR, _C_END = 0, 256, 512, 896, 1152, 1408, 1536


def _rms(x):
    return x * lax.rsqrt(jnp.mean(x * x, axis=-1, keepdims=True) + EPS)


def _cparams(sem, limit=VMEM_LIMIT):
    return pltpu.CompilerParams(dimension_semantics=sem, vmem_limit_bytes=limit)


def _mod_kernel(c_ref, w_ref, b_ref, o_ref):
    c = c_ref[...]
    ca = c * jax.nn.sigmoid(c)
    o_ref[0] = jnp.dot(ca, w_ref[0], preferred_element_type=F32) + b_ref[0]


def _modulation(c_pad, w_mod, b_mod):
    L, D, six_d = w_mod.shape
    nb = six_d // D
    rows = c_pad.shape[0]
    return pl.pallas_call(
        _mod_kernel,
        grid=(L, nb),
        in_specs=[
            pl.BlockSpec((rows, D), lambda l, j: (0, 0)),
            pl.BlockSpec((1, D, D), lambda l, j: (l, 0, j)),
            pl.BlockSpec((1, 1, D), lambda l, j: (l, 0, j)),
        ],
        out_specs=pl.BlockSpec((1, rows, D), lambda l, j: (l, 0, j)),
        out_shape=jax.ShapeDtypeStruct((L, rows, six_d), F32),
        compiler_params=_cparams(("arbitrary", "arbitrary")),
        name="modulation",
    )(c_pad, w_mod, b_mod.reshape(L, 1, six_d))


def _rope_kernel(ang_ref, cos_ref, sin_ref):
    ang = ang_ref[...]
    lane = lax.broadcasted_iota(jnp.int32, ang.shape, 1)
    first_half = lane < QK_NOPE + QK_ROPE // 2
    cos_ref[...] = jnp.cos(ang)
    s = jnp.sin(ang)
    sin_ref[...] = jnp.where(first_half, -s, s)


def _rope_tables(ang, tb):
    T = ang.shape[0]
    spec = pl.BlockSpec((tb, LANES), lambda i: (i, 0))
    return pl.pallas_call(
        _rope_kernel,
        grid=(T // tb,),
        in_specs=[spec],
        out_specs=[spec, spec],
        out_shape=[jax.ShapeDtypeStruct((T, LANES), F32)] * 2,
        compiler_params=_cparams(("arbitrary",)),
        name="rope_tables",
    )(ang)


def _rope_apply(xh, cos, sin):
    lane = lax.broadcasted_iota(jnp.int32, xh.shape, 1)
    half = QK_ROPE // 2
    partner = jnp.where(lane < QK_NOPE + half,
                        pltpu.roll(xh, LANES - half, axis=1),
                        pltpu.roll(xh, half, axis=1))
    return xh * cos + partner * sin


def _inproj_kernel(x_ref, xh_ref, mod_ref, w1_ref, ws_ref, bs_ref, qn_ref, wuq_ref, kvn_ref,
                   wk_ref, wv_ref, cos_ref, sin_ref, wpool_ref, pscale_ref,
                   ya_ref, yc_ref, q_ref, k_ref, v_ref, *, seq, tb):
    i = pl.program_id(0)
    m = mod_ref[0]
    sh1, sc1 = m[0:1], m[1:2]
    h = (_rms(x_ref[...]) * (1.0 + sc1) + sh1).astype(BF16)
    z = jnp.dot(h, w1_ref[...], preferred_element_type=F32)

    gu = jax.nn.gelu(z[:, _C_ZU:_C_ZV])
    gv = jax.nn.gelu(z[:, _C_ZV:_C_CQ])
    mu = jnp.mean(gv, axis=-1, keepdims=True)
    dv = gv - mu
    vn = (dv * lax.rsqrt(jnp.mean(dv * dv, axis=-1, keepdims=True) + EPS)).astype(BF16)
    r = lax.broadcasted_iota(jnp.int32, (A_CHUNK, A_CHUNK), 0)
    cc = lax.broadcasted_iota(jnp.int32, (A_CHUNK, A_CHUNK), 1)
    lane_a = lax.broadcasted_iota(jnp.int32, (A_CHUNK, A_WIDTH), 1)
    wtril = [jnp.where(r >= cc, ws_ref[g], 0.0).astype(BF16) for g in range(A_GROUPS)]
    for ci in range(tb // A_CHUNK):
        lo = ci * A_CHUNK
        vc = vn[lo:lo + A_CHUNK]
        mixed = bs_ref[...]
        for g in range(A_GROUPS):
            dg = jnp.dot(wtril[g], vc, preferred_element_type=F32)
            in_g = (lane_a >= g * A_GROUP_DIM) & (lane_a < (g + 1) * A_GROUP_DIM)
            mixed = mixed + jnp.where(in_g, dg, 0.0)
        ya_ref[lo:lo + A_CHUNK, :] = (gu[lo:lo + A_CHUNK] * mixed).astype(BF16)

    zp = z[:, _C_ZP:_C_KR]
    t0 = (i * tb) % seq
    hh = (_rms(xh_ref[...]) * (1.0 + sc1) + sh1).astype(BF16)
    zph = jnp.dot(hh, w1_ref[:, _C_ZP:_C_KR], preferred_element_type=F32)
    zph = zph * jnp.where(t0 == 0, 0.0, 1.0)
    zext = jnp.concatenate([zph, zp], axis=0)
    n_ext = HALO + tb
    s2 = zext + pltpu.roll(zext, 1, axis=0)
    s4 = s2 + pltpu.roll(s2, 2, axis=0)
    s8 = s4 + pltpu.roll(s4, 4, axis=0)
    s16 = s8 + pltpu.roll(s8, 8, axis=0)
    lane_c = lax.broadcasted_iota(jnp.int32, (n_ext, C_WIDTH), 1)
    grp = lane_c // C_GROUP
    win = jnp.where(grp == 0, s2, jnp.where(grp == 1, s4, jnp.where(grp == 2, s8, s16)))
    wsize = jnp.where(grp == 0, 2, jnp.where(grp == 1, 4, jnp.where(grp == 2, 8, 16)))
    tpos = t0 - HALO + lax.broadcasted_iota(jnp.int32, (n_ext, C_WIDTH), 0)
    cnt = jnp.minimum(tpos + 1, wsize).astype(F32)
    pooled = (win / jnp.maximum(cnt, 1.0) - zext)[HALO:]
    yc = jnp.dot(pooled.astype(BF16), wpool_ref[...], preferred_element_type=F32)
    yc_ref[...] = (yc * pscale_ref[...]).astype(BF16)

    cos = cos_ref[...]
    sin = sin_ref[...]
    cqn = (_rms(z[:, _C_CQ:_C_CKV]) * qn_ref[...]).astype(BF16)
    q = jnp.dot(cqn, wuq_ref[...], preferred_element_type=F32)
    ckvn = (_rms(z[:, _C_CKV:_C_ZP]) * kvn_ref[...]).astype(BF16)
    kn = jnp.dot(ckvn, wk_ref[...], preferred_element_type=F32)
    vv = jnp.dot(ckvn, wv_ref[...], preferred_element_type=F32)
    lane_v = lax.broadcasted_iota(jnp.int32, vv.shape, 1)
    v_ref[...] = jnp.where(lane_v % HEAD_PAD == ONES_LANE, 1.0, vv).astype(BF16)
    krope = _rope_apply(z[:, _C_KR:_C_END], cos, sin)
    qscale = math.log2(math.e) / math.sqrt(QK_DIM)
    for hd in range(MLA_HEADS):
        sl = slice(hd * HEAD_PAD, (hd + 1) * HEAD_PAD)
        q_ref[:, sl] = (_rope_apply(q[:, sl], cos, sin) * qscale).astype(BF16)
        k_ref[:, sl] = (kn[:, sl] + krope).astype(BF16)


def _inproj(x, mod, w1, w_s, bs_full, q_norm, wuq, kv_norm, wk, wv, cos, sin, wpool, pscale, *, seq, tb):
    T, D = x.shape
    hpb = tb // HALO
    kern = functools.partial(_inproj_kernel, seq=seq, tb=tb)
    full = lambda a: pl.BlockSpec(a.shape, lambda i: (0,) * a.ndim)
    hw = MLA_HEADS * HEAD_PAD
    return pl.pallas_call(
        kern,
        grid=(T // tb,),
        in_specs=[
            pl.BlockSpec((tb, D), lambda i: (i, 0)),
            pl.BlockSpec((HALO, D), lambda i: (jnp.maximum(i * hpb - 1, 0), 0)),
            pl.BlockSpec((1, SUBLANES, D), lambda i: ((i * tb) // seq, 0, 0)),
            full(w1), full(w_s), full(bs_full), full(q_norm), full(wuq), full(kv_norm),
            full(wk), full(wv),
            pl.BlockSpec((tb, LANES), lambda i: (i, 0)),
            pl.BlockSpec((tb, LANES), lambda i: (i, 0)),
            full(wpool), full(pscale),
        ],
        out_specs=[
            pl.BlockSpec((tb, A_WIDTH), lambda i: (i, 0)),
            pl.BlockSpec((tb, C_WIDTH), lambda i: (i, 0)),
            pl.BlockSpec((tb, hw), lambda i: (i, 0)),
            pl.BlockSpec((tb, hw), lambda i: (i, 0)),
            pl.BlockSpec((tb, hw), lambda i: (i, 0)),
        ],
        out_shape=[
            jax.ShapeDtypeStruct((T, A_WIDTH), BF16),
            jax.ShapeDtypeStruct((T, C_WIDTH), BF16),
            jax.ShapeDtypeStruct((T, hw), BF16),
            jax.ShapeDtypeStruct((T, hw), BF16),
            jax.ShapeDtypeStruct((T, hw), BF16),
        ],
        compiler_params=_cparams(("arbitrary",)),
        name="inproj",
    )(x, x, mod, w1, w_s, bs_full, q_norm, wuq, kv_norm, wk, wv, cos, sin, wpool, pscale)


def _attn_kernel(q_ref, kt_ref, v_ref, o_ref, m_scr, acc_scr, *, ta):
    qi = pl.program_id(2)
    ki = pl.program_id(3)

    @pl.when(ki == 0)
    def _():
        m_scr[...] = jnp.full(m_scr.shape, NEG, F32)
        acc_scr[...] = jnp.zeros(acc_scr.shape, F32)

    def block(diagonal):
        n_chunks = ATT_CHUNKS if ta % (ATT_CHUNKS * SUBLANES) == 0 else 1
        rows = ta // n_chunks
        sls = [slice(h * rows, (h + 1) * rows) for h in range(n_chunks)]
        m_prevs = [m_scr[sl, :] for sl in sls]
        acc_prevs = [acc_scr[sl, :] for sl in sls]

        def n_cols(h):
            return (h + 1) * rows if diagonal else ta

        def scores(h):
            s = jnp.dot(q_ref[sls[h], :], kt_ref[:, :n_cols(h)], preferred_element_type=F32)
            if diagonal:
                row = h * rows + lax.broadcasted_iota(jnp.int32, s.shape, 0)
                col = lax.broadcasted_iota(jnp.int32, s.shape, 1)
                s = jnp.where(col <= row, s, NEG)
            return s

        def softmax(h, s):
            m_new = jnp.maximum(m_prevs[h], jnp.max(s, axis=1, keepdims=True))
            alpha = jnp.exp2(m_prevs[h] - m_new)
            return m_new, alpha, jnp.exp2(s - m_new[:, 0:1]).astype(BF16)

        def weighted(h, alpha, p):
            return alpha * acc_prevs[h] + jnp.dot(p, v_ref[:n_cols(h), :], preferred_element_type=F32)

        s_next = scores(0)
        pending = None
        done = []
        for h in range(n_chunks + 1):
            if h < n_chunks:
                s_cur = s_next
                if h + 1 < n_chunks:
                    s_next = scores(h + 1)
                m_new, alpha, p = softmax(h, s_cur)
            if pending is not None:
                hp, m_p, alpha_p, p_p = pending
                done.append((hp, m_p, weighted(hp, alpha_p, p_p)))
            pending = (h, m_new, alpha, p)
        for hp, m_p, acc in done:
            m_scr[sls[hp], :] = m_p
            acc_scr[sls[hp], :] = acc

    @pl.when(ki < qi)
    def _():
        block(False)

    @pl.when(ki == qi)
    def _():
        block(True)
        acc = acc_scr[...]
        o_ref[...] = (acc / acc[:, ONES_LANE:ONES_LANE + 1]).astype(BF16)


def _attention(q, kt, v, *, batch, seq, ta):
    T, hw = q.shape
    nq = seq // ta
    kern = functools.partial(_attn_kernel, ta=ta)
    qspec = pl.BlockSpec((ta, HEAD_PAD), lambda b, h, qi, ki: (b * nq + qi, h))
    kspec = pl.BlockSpec((ta, HEAD_PAD), lambda b, h, qi, ki: (b * nq + jnp.minimum(ki, qi), h))
    ktspec = pl.BlockSpec((HEAD_PAD, ta), lambda b, h, qi, ki: (h, b * nq + jnp.minimum(ki, qi)))
    return pl.pallas_call(
        kern,
        grid=(batch, MLA_HEADS, nq, nq),
        in_specs=[qspec, ktspec, kspec],
        out_specs=qspec,
        out_shape=jax.ShapeDtypeStruct((T, hw), BF16),
        scratch_shapes=[pltpu.VMEM((ta, HEAD_PAD), F32)] * 2,
        compiler_params=_cparams(("arbitrary",) * 4),
        name="attention",
    )(q, kt, v)


def _merge_kernel(x_ref, mod_ref, ya_ref, yb_ref, yc_ref, wg_ref, wa_ref, wb_ref, wc_ref, wo_ref,
                  xo_ref):
    D = D_MODEL
    x = x_ref[...]
    m = mod_ref[0]
    sh1, sc1, g1 = m[0:1], m[1:2], m[2:3]
    h = (_rms(x) * (1.0 + sc1) + sh1).astype(BF16)
    gates = jax.nn.sigmoid(jnp.dot(h, wg_ref[...], preferred_element_type=F32))
    merged = gates[:, 0:D] * jnp.dot(ya_ref[...], wa_ref[...], preferred_element_type=F32)
    merged += gates[:, D:2 * D] * jnp.dot(yb_ref[...], wb_ref[...], preferred_element_type=F32)
    merged += gates[:, 2 * D:3 * D] * jnp.dot(yc_ref[...], wc_ref[...], preferred_element_type=F32)
    xo_ref[...] = x + g1 * jnp.dot(merged.astype(BF16), wo_ref[...], preferred_element_type=F32)


def _merge(x, mod, ya, yb, yc, wg, wa, wb, wc, wo, *, seq, tb):
    T, D = x.shape
    full = lambda a: pl.BlockSpec(a.shape, lambda i: (0,) * a.ndim)
    row = lambda w: pl.BlockSpec((tb, w), lambda i: (i, 0))
    return pl.pallas_call(
        _merge_kernel,
        grid=(T // tb,),
        in_specs=[row(D), pl.BlockSpec((1, SUBLANES, D), lambda i: ((i * tb) // seq, 0, 0)),
                  row(ya.shape[1]), row(yb.shape[1]), row(yc.shape[1]),
                  full(wg), full(wa), full(wb), full(wc), full(wo)],
        out_specs=row(D),
        out_shape=jax.ShapeDtypeStruct((T, D), F32),
        compiler_params=_cparams(("arbitrary",)),
        name="merge",
    )(x, mod, ya, yb, yc, wg, wa, wb, wc, wo)


def _top16_rows(s, vals_ref, idx_ref, payload=None):
    _top16_lockstep([(s, vals_ref, idx_ref, payload)])


def _top16_lockstep(problems):
    state = [p[0] for p in problems]
    rows = lax.broadcasted_iota(jnp.int32, state[0].shape, 0).astype(F32)
    for j in range(PEER_TOPK):
        for i, (_, vals_ref, idx_ref, payload) in enumerate(problems):
            s = state[i]
            mx = jnp.max(s, axis=0, keepdims=True)
            pos = jnp.min(jnp.where(s == mx, rows, float(s.shape[0])), axis=0, keepdims=True)
            hit = rows == pos
            vals_ref[j:j + 1, :] = mx
            if payload is None:
                idx_ref[j:j + 1, :] = pos
            else:
                picked = jnp.sum(jnp.where(hit, payload, 0.0), axis=0, keepdims=True)
                idx_ref[j:j + 1, :] = picked.astype(jnp.int32)
            state[i] = jnp.where(hit, NEG, s)


def _route_kernel(x_ref, mod_ref, wpq_ref, keys_ref, h2_ref, idx_ref, g_ref,
                  h2_scr, v1_scr, i1_scr, v2_scr, i2_scr, tv_scr):
    hd = pl.program_id(1)

    @pl.when(hd == 0)
    def _():
        m = mod_ref[0]
        sh2, sc2 = m[3:4], m[4:5]
        h2 = _rms(x_ref[...]) * (1.0 + sc2) + sh2
        h2_ref[...] = h2
        h2_scr[...] = h2.astype(BF16)

    q = jnp.dot(h2_scr[...], wpq_ref[...], preferred_element_type=F32).astype(BF16)
    nt = (((1,), (1,)), ((), ()))
    s1 = lax.dot_general(keys_ref[0, 0], q[:, :PEER_HALF], nt, preferred_element_type=F32)
    s2 = lax.dot_general(keys_ref[0, 1], q[:, PEER_HALF:], nt, preferred_element_type=F32)
    _top16_lockstep([(s1, v1_scr, i1_scr, None), (s2, v2_scr, i2_scr, None)])
    v1, v2 = v1_scr[...], v2_scr[...]
    i1, i2 = i1_scr[...] * N_KEYS, i2_scr[...]

    r8 = lax.broadcasted_iota(jnp.int32, (SUBLANES, v1.shape[1]), 0)
    low = r8 < 4

    def two(a0, a1, nb0, nb1, val, idx):
        va = jnp.where(low, val[a0:a0 + 1], val[a1:a1 + 1])
        ia = jnp.where(low, idx[a0:a0 + 1], idx[a1:a1 + 1])
        ok = r8 < jnp.where(low, nb0, nb1 + 4)
        return va, ia, ok

    v2lo, i2lo = v2[0:SUBLANES], i2[0:SUBLANES]
    v2rep = jnp.where(low, v2lo, pltpu.roll(v2lo, 4, axis=0))
    i2rep = jnp.where(low, i2lo, pltpu.roll(i2lo, 4, axis=0))
    cand = [v1[0:1] + v2[0:SUBLANES], v1[0:1] + v2[SUBLANES:], v1[1:2] + v2lo]
    cidx = [i1[0:1] + i2[0:SUBLANES], i1[0:1] + i2[SUBLANES:], i1[1:2] + i2lo]
    for a, nb in ((2, 5), (3, 4)):
        cand.append(jnp.where(r8 < nb, v1[a:a + 1] + v2lo, NEG))
        cidx.append(i1[a:a + 1] + i2lo)
    for a0, a1, nb0, nb1 in ((4, 5, 3, 2), (6, 7, 2, 2)):
        va, ia, ok = two(a0, a1, nb0, nb1, v1, i1)
        cand.append(jnp.where(ok, va + v2rep, NEG))
        cidx.append(ia + i2rep)
    cand.append(v1[SUBLANES:] + v2[0:1])
    cidx.append(i1[SUBLANES:] + i2[0:1])
    cand = jnp.concatenate(cand, axis=0)
    cidx = jnp.concatenate(cidx, axis=0)
    _top16_rows(cand, tv_scr, idx_ref, payload=cidx)
    tv = tv_scr[...]
    e = jnp.exp(tv - tv[0:1])
    g_ref[...] = e / jnp.sum(e, axis=0, keepdims=True)


def _route(x, mod, wpq, keys, *, seq, tb):
    T, D = x.shape
    kd = 2 * PEER_HALF
    return pl.pallas_call(
        _route_kernel,
        grid=(T // tb, PEER_HEADS),
        in_specs=[
            pl.BlockSpec((tb, D), lambda i, h: (i, 0)),
            pl.BlockSpec((1, SUBLANES, D), lambda i, h: ((i * tb) // seq, 0, 0)),
            pl.BlockSpec((D, kd), lambda i, h: (0, h)),
            pl.BlockSpec((1, 2, N_KEYS, PEER_HALF), lambda i, h: (h, 0, 0, 0)),
        ],
        out_specs=[
            pl.BlockSpec((tb, D), lambda i, h: (i, 0)),
            pl.BlockSpec((PEER_TOPK, tb), lambda i, h: (h, i)),
            pl.BlockSpec((PEER_TOPK, tb), lambda i, h: (h, i)),
        ],
        out_shape=[
            jax.ShapeDtypeStruct((T, D), F32),
            jax.ShapeDtypeStruct((N_PAIRS, T), jnp.int32),
            jax.ShapeDtypeStruct((N_PAIRS, T), F32),
        ],
        scratch_shapes=[
            pltpu.VMEM((tb, D), BF16),
            pltpu.VMEM((PEER_TOPK, tb), F32), pltpu.VMEM((PEER_TOPK, tb), F32),
            pltpu.VMEM((PEER_TOPK, tb), F32), pltpu.VMEM((PEER_TOPK, tb), F32),
            pltpu.VMEM((PEER_TOPK, tb), F32),
        ],
        compiler_params=_cparams(("arbitrary", "arbitrary")),
        name="route",
    )(x, mod, wpq, keys)


def _load_table(tab_hbm, tab_vmem, sem):
    @pl.when(pl.program_id(0) == 0)
    def _():
        cp = pltpu.make_async_copy(tab_hbm, tab_vmem, sem)
        cp.start()
        cp.wait()


HALF_TILE = SUBLANES // 2
HALF_PAIRS = N_PAIRS // 2
W_ROWS = HALF_PAIRS * SUBLANES


def _pair_tiles(tab_vmem, rows_ref, t):
    tiles = []
    for c in range(N_PAIRS // ROW_CHUNK):
        rows_c = rows_ref.at[pl.ds(t * N_PAIRS + c * ROW_CHUNK, ROW_CHUNK)]
        tiles += [tab_vmem[pl.ds(pl.multiple_of(rows_c[j], HALF_TILE), HALF_TILE), :] for j in range(ROW_CHUNK)]
    wa = pltpu.bitcast(jnp.concatenate(tiles[:HALF_PAIRS], axis=0), BF16)
    wb = pltpu.bitcast(jnp.concatenate(tiles[HALF_PAIRS:], axis=0), BF16)
    return jnp.concatenate([wa, wb], axis=1)


ROW_CHUNK = 16
TOKENS_PER_TRIP = 16


def _token_loop(tb, token):
    def trip(i, carry):
        for u in range(TOKENS_PER_TRIP):
            token(i * TOKENS_PER_TRIP + u, carry)
        return carry
    lax.fori_loop(0, tb // TOKENS_PER_TRIP, trip, 0)


def _expansion_constants():
    col = np.arange(2 * W_ROWS)
    pair = (col // W_ROWS) * HALF_PAIRS + (col % W_ROWS) // SUBLANES
    g = np.zeros((2 * W_ROWS, N_PAIRS), np.float32)
    g[col, pair] = 1.0
    return g, g.T.copy()


def _peer_u_kernel(rows_ref, h_ref, g_ref, gsum_ref, spread_ref, tab_hbm, cexp_ref,
                   tab_vmem, stage, sem, *, tb):
    _load_table(tab_hbm, tab_vmem, sem)
    sub = lax.broadcasted_iota(jnp.int32, (2 * SUBLANES, W_ROWS), 0)
    lane = lax.broadcasted_iota(jnp.int32, (2 * SUBLANES, W_ROWS), 1)
    diag = (lane & (SUBLANES - 1)) == (sub & (SUBLANES - 1))
    nt = (((1,), (1,)), ((), ()))

    def token(t, carry):
        w = _pair_tiles(tab_vmem, rows_ref, t)
        hrow = h_ref[pl.ds(t, 1), :]
        xt = jnp.concatenate([hrow[:, r * LANES:(r + 1) * LANES] for r in range(SUBLANES)], axis=0).astype(BF16)
        z = jnp.zeros_like(xt)
        x2 = jnp.concatenate([jnp.concatenate([xt, z], axis=1),
                              jnp.concatenate([z, xt], axis=1)], axis=0)
        res = lax.dot_general(x2, w, nt, preferred_element_type=F32)
        m = jnp.where(diag, res, 0.0)
        stage[pl.ds(t, 1), 0:W_ROWS] = jnp.sum(m[:SUBLANES], axis=0, keepdims=True)
        stage[pl.ds(t, 1), W_ROWS:] = jnp.sum(m[SUBLANES:], axis=0, keepdims=True)
        return carry

    _token_loop(tb, token)
    s = stage[...]
    hi = s.astype(BF16)
    lo = (s - hi.astype(F32)).astype(BF16)
    a = (jnp.dot(hi, gsum_ref[...], preferred_element_type=F32)
         + jnp.dot(lo, gsum_ref[...], preferred_element_type=F32))
    coef = (g_ref[...] * jax.nn.gelu(a)).astype(BF16)
    cexp_ref[...] = jnp.dot(coef, spread_ref[...], preferred_element_type=F32)


def _peer_u(rows_tm, h2, g_tm, gsum, spread, tab, *, tb):
    T = h2.shape[0]
    kern = functools.partial(_peer_u_kernel, tb=tb)
    full = lambda a: pl.BlockSpec(a.shape, lambda i: (0,) * a.ndim)
    return pl.pallas_call(
        kern,
        grid=(T // tb,),
        in_specs=[
            pl.BlockSpec((tb * N_PAIRS,), lambda i: (i,), memory_space=pltpu.SMEM),
            pl.BlockSpec((tb, D_MODEL), lambda i: (i, 0)),
            pl.BlockSpec((tb, N_PAIRS), lambda i: (i, 0)),
            full(gsum), full(spread),
            pl.BlockSpec(memory_space=pl.ANY),
        ],
        out_specs=pl.BlockSpec((tb, 2 * W_ROWS), lambda i: (i, 0)),
        out_shape=jax.ShapeDtypeStruct((T, 2 * W_ROWS), F32),
        scratch_shapes=[
            pltpu.VMEM((N_EXPERTS * HALF_TILE, LANES), jnp.uint32),
            pltpu.VMEM((tb, 2 * W_ROWS), F32),
            pltpu.SemaphoreType.DMA,
        ],
        compiler_params=_cparams(("arbitrary",), TABLE_VMEM_LIMIT),
        name="peer_u",
    )(rows_tm, h2, g_tm, gsum, spread, tab)


def _peer_v_kernel(rows_ref, cexp_ref, x_ref, g2_ref, tab_hbm, xo_ref, tab_vmem, sem, *, tb):
    _load_table(tab_hbm, tab_vmem, sem)
    g2 = g2_ref[0]
    sub = lax.broadcasted_iota(jnp.int32, (SUBLANES, LANES), 0)
    lane = lax.broadcasted_iota(jnp.int32, (SUBLANES, LANES), 1)
    diag = (lane & (SUBLANES - 1)) == sub
    n_chunks = W_ROWS // LANES

    def token(t, carry):
        w = _pair_tiles(tab_vmem, rows_ref, t)
        crow = cexp_ref[pl.ds(t, 1), :]
        blocks = []
        for blk in range(2):
            lo = [(blk * n_chunks + q) * LANES for q in range(n_chunks)]
            chunks = [jnp.where(diag, jnp.broadcast_to(crow[:, c:c + LANES], diag.shape), 0.0) for c in lo]
            blocks.append(jnp.concatenate(chunks, axis=1))
        c = jnp.concatenate(blocks, axis=0).astype(BF16)
        res = jnp.dot(c, w, preferred_element_type=F32)
        out = g2 * (res[:SUBLANES, :LANES] + res[SUBLANES:, LANES:])
        delta = jnp.concatenate([out[r:r + 1, :] for r in range(SUBLANES)], axis=1)
        xo_ref[pl.ds(t, 1), :] = x_ref[pl.ds(t, 1), :] + delta
        return carry

    _token_loop(tb, token)


def _peer_v(rows_tm, cexp, x, g2, tab, *, seq, tb):
    T = x.shape[0]
    kern = functools.partial(_peer_v_kernel, tb=tb)
    tok = pl.BlockSpec((tb, D_MODEL), lambda i: (i, 0))
    return pl.pallas_call(
        kern,
        grid=(T // tb,),
        in_specs=[
            pl.BlockSpec((tb * N_PAIRS,), lambda i: (i,), memory_space=pltpu.SMEM),
            pl.BlockSpec((tb, 2 * W_ROWS), lambda i: (i, 0)),
            tok,
            pl.BlockSpec((1, SUBLANES, LANES), lambda i: ((i * tb) // seq, 0, 0)),
            pl.BlockSpec(memory_space=pl.ANY),
        ],
        out_specs=tok,
        out_shape=jax.ShapeDtypeStruct(x.shape, F32),
        scratch_shapes=[
            pltpu.VMEM((N_EXPERTS * HALF_TILE, LANES), jnp.uint32),
            pltpu.SemaphoreType.DMA,
        ],
        compiler_params=_cparams(("arbitrary",), TABLE_VMEM_LIMIT),
        name="peer_v",
    )(rows_tm, cexp, x, g2, tab)


def _final_kernel(x_ref, g_ref, o_ref):
    o_ref[...] = _rms(x_ref[...]) * g_ref[...]


def _final_norm(x, gain, tb):
    T, D = x.shape
    return pl.pallas_call(
        _final_kernel,
        grid=(T // tb,),
        in_specs=[pl.BlockSpec((tb, D), lambda i: (i, 0)), pl.BlockSpec((1, D), lambda i: (0, 0))],
        out_specs=pl.BlockSpec((tb, D), lambda i: (i, 0)),
        out_shape=jax.ShapeDtypeStruct((T, D), F32),
        compiler_params=_cparams(("arbitrary",)),
        name="final_norm",
    )(x, gain)


def _pack_table(tab):
    lead = tab.shape[:-2]
    pairs = tab.astype(BF16).reshape(*lead, N_EXPERTS, HALF_TILE, 2, LANES)
    packed = lax.bitcast_convert_type(jnp.swapaxes(pairs, -1, -2), jnp.uint32)
    return packed.reshape(*lead, N_EXPERTS * HALF_TILE, LANES)


def _pad_heads(w, per_head, offset):
    k = w.shape[0]
    w = w.reshape(k, MLA_HEADS, per_head)
    out = jnp.zeros((k, MLA_HEADS, HEAD_PAD), w.dtype)
    out = out.at[:, :, offset:offset + per_head].set(w)
    return out.reshape(k, MLA_HEADS * HEAD_PAD)


def _block_size(seq, want):
    tb = min(want, seq)
    assert seq % tb == 0
    return tb


def _layer(xf, mod_l, cos, sin, w_in, w_s, b_s, q_norm, w_uq, kv_norm, w_ukv, w_pool, pool_scale,
           w_a, w_b, w_c, w_o, w_pq, sub_keys, u_packed, v_packed, *, batch, seq):
    T, D = xf.shape
    tb_in = _block_size(seq, 512)
    tb_mg = _block_size(seq, 256)
    tb_rt = _block_size(seq, 1024)
    ta = _block_size(seq, 2048)
    tb_peer = LANES

    sp = (0, 256, 512, 896, 1152, 1184, 1440)
    kr_pad = jnp.zeros((D, LANES), F32).at[:, QK_NOPE:QK_NOPE + QK_ROPE].set(w_in[:, sp[4]:sp[5]])
    w1 = jnp.concatenate([w_in[:, :sp[4]], w_in[:, sp[5]:sp[6]], kr_pad], axis=1).astype(BF16)
    wg = w_in[:, sp[6]:].astype(BF16)
    bs_full = jnp.repeat(b_s.T, A_GROUP_DIM, axis=1)
    wuq = _pad_heads(w_uq, QK_DIM, 0).astype(BF16)
    wukv = w_ukv.reshape(KV_LORA, MLA_HEADS, QK_NOPE + V_DIM)
    wk = _pad_heads(wukv[:, :, :QK_NOPE].reshape(KV_LORA, -1), QK_NOPE, 0).astype(BF16)
    wv = _pad_heads(wukv[:, :, QK_NOPE:].reshape(KV_LORA, -1), V_DIM, 0).astype(BF16)
    wpool = jax.scipy.linalg.block_diag(*[w_pool[g] for g in range(len(POOL_WINDOWS))]).astype(BF16)
    wb = w_b.reshape(MLA_HEADS, V_DIM, D)
    wb = jnp.pad(wb, ((0, 0), (0, HEAD_PAD - V_DIM), (0, 0))).reshape(MLA_HEADS * HEAD_PAD, D).astype(BF16)

    ya, yc, q, k, v = _inproj(
        xf, mod_l, w1, w_s, bs_full, q_norm.reshape(1, -1), wuq, kv_norm.reshape(1, -1),
        wk, wv, cos, sin, wpool, pool_scale.reshape(1, -1), seq=seq, tb=tb_in)
    yb = _attention(q, k.T, v, batch=batch, seq=seq, ta=ta)
    x1 = _merge(xf, mod_l, ya, yb, yc, wg, w_a.astype(BF16), wb, w_c.astype(BF16),
                w_o.astype(BF16), seq=seq, tb=tb_mg)
    h2, idx_t, g_t = _route(x1, mod_l, w_pq.astype(BF16), sub_keys.astype(BF16), seq=seq, tb=tb_rt)
    rows_tm = (idx_t.T * HALF_TILE).reshape(-1)
    gsum, spread = (jnp.asarray(a, BF16) for a in _expansion_constants())
    cexp = _peer_u(rows_tm, h2, g_t.T, gsum, spread,
                   u_packed, tb=tb_peer)
    g2 = mod_l[:, 5].reshape(batch, SUBLANES, LANES)
    x2 = _peer_v(rows_tm, cexp, x1, g2, v_packed, seq=seq, tb=tb_peer)
    return x2, (x1, h2)


def kernel(x, c, positions, w_mod, b_mod, w_in, w_s, b_s, q_norm, w_uq, kv_norm, w_ukv, w_pool, pool_scale, w_a, w_b, w_c, w_o, w_pq, sub_keys, u_tab, v_tab, final_norm):
    B, S, D = x.shape
    L = w_mod.shape[0]
    T = B * S
    assert D == D_MODEL and S % A_CHUNK == 0

    c_pad = jnp.zeros((SUBLANES, D), F32).at[:B].set(c)
    mod = _modulation(c_pad, w_mod, b_mod)[:, :B].reshape(L, B, 6, D)
    mod = jnp.pad(mod, ((0, 0), (0, 0), (0, SUBLANES - 6), (0, 0)))

    inv_freq = 1.0 / (ROPE_THETA ** (jnp.arange(0, QK_ROPE, 2, dtype=F32) / QK_ROPE))
    half = QK_ROPE // 2
    freq_lane = jnp.zeros((LANES,), F32)
    freq_lane = freq_lane.at[QK_NOPE:QK_NOPE + half].set(inv_freq)
    freq_lane = freq_lane.at[QK_NOPE + half:QK_NOPE + QK_ROPE].set(inv_freq)
    ang = positions.astype(F32).reshape(T, 1) * freq_lane[None, :]
    cos, sin = _rope_tables(ang, _block_size(T, 1024))

    u_packed = _pack_table(u_tab)
    v_packed = _pack_table(v_tab)
    xf = x.reshape(T, D)
    for l in range(L):
        xf, _ = _layer(xf, mod[l], cos, sin, w_in[l], w_s[l], b_s[l], q_norm[l], w_uq[l], kv_norm[l],
                       w_ukv[l], w_pool[l], pool_scale[l], w_a[l], w_b[l], w_c[l], w_o[l], w_pq[l],
                       sub_keys[l], u_packed[l], v_packed[l], batch=B, seq=S)

    out = _final_norm(xf, final_norm.reshape(1, D), _block_size(T, 1024))
    return out.reshape(B, S, D)
```

```python
import functools
import math

import numpy as np
import jax
import jax.numpy as jnp
from jax import lax
from jax.experimental import pallas as pl
from jax.experimental.pallas import tpu as pltpu

F32 = jnp.float32
BF16 = jnp.bfloat16

EPS = 1e-6
D_MODEL = 1024
A_WIDTH = 256
A_GROUPS = 4
A_GROUP_DIM = A_WIDTH // A_GROUPS
A_CHUNK = 128
MLA_HEADS = 8
Q_LORA = 384
KV_LORA = 256
QK_NOPE = 64
QK_ROPE = 32
V_DIM = 64
QK_DIM = QK_NOPE + QK_ROPE
ROPE_THETA = 10000.0
POOL_WINDOWS = (2, 4, 8, 16)
C_WIDTH = 256
C_GROUP = C_WIDTH // len(POOL_WINDOWS)
PEER_HEADS = 8
N_KEYS = 128
N_EXPERTS = N_KEYS * N_KEYS
PEER_HALF = 128
PEER_TOPK = 16
N_PAIRS = PEER_HEADS * PEER_TOPK

LANES = 128
SUBLANES = 8
HEAD_PAD = LANES
HALO = 16
NEG = -1e30
ATT_CHUNKS = 8
ONES_LANE = V_DIM

VMEM_LIMIT = 48 * 1024 * 1024
TABLE_VMEM_LIMIT = 56 * 1024 * 1024

_C_ZU, _C_ZV, _C_CQ, _C_CKV, _C_ZP, _C_KR, _C_END = 0, 256, 512, 896, 1152, 1408, 1536


def _rms(x):
    return x * lax.rsqrt(jnp.mean(x * x, axis=-1, keepdims=True) + EPS)


def _cparams(sem, limit=VMEM_LIMIT):
    return pltpu.CompilerParams(dimension_semantics=sem, vmem_limit_bytes=limit)


def _mod_kernel(c_ref, w_ref, b_ref, o_ref):
    c = c_ref[...]
    ca = c * jax.nn.sigmoid(c)
    o_ref[0] = jnp.dot(ca, w_ref[0], preferred_element_type=F32) + b_ref[0]


def _modulation(c_pad, w_mod, b_mod):
    L, D, six_d = w_mod.shape
    nb = six_d // D
    rows = c_pad.shape[0]
    return pl.pallas_call(
        _mod_kernel,
        grid=(L, nb),
        in_specs=[
            pl.BlockSpec((rows, D), lambda l, j: (0, 0)),
            pl.BlockSpec((1, D, D), lambda l, j: (l, 0, j)),
            pl.BlockSpec((1, 1, D), lambda l, j: (l, 0, j)),
        ],
        out_specs=pl.BlockSpec((1, rows, D), lambda l, j: (l, 0, j)),
        out_shape=jax.ShapeDtypeStruct((L, rows, six_d), F32),
        compiler_params=_cparams(("arbitrary", "arbitrary")),
        name="modulation",
    )(c_pad, w_mod, b_mod.reshape(L, 1, six_d))


def _rope_kernel(ang_ref, cos_ref, sin_ref):
    ang = ang_ref[...]
    lane = lax.broadcasted_iota(jnp.int32, ang.shape, 1)
    first_half = lane < QK_NOPE + QK_ROPE // 2
    cos_ref[...] = jnp.cos(ang)
    s = jnp.sin(ang)
    sin_ref[...] = jnp.where(first_half, -s, s)


def _rope_tables(ang, tb):
    T = ang.shape[0]
    spec = pl.BlockSpec((tb, LANES), lambda i: (i, 0))
    return pl.pallas_call(
        _rope_kernel,
        grid=(T // tb,),
        in_specs=[spec],
        out_specs=[spec, spec],
        out_shape=[jax.ShapeDtypeStruct((T, LANES), F32)] * 2,
        compiler_params=_cparams(("arbitrary",)),
        name="rope_tables",
    )(ang)


def _rope_apply(xh, cos, sin):
    lane = lax.broadcasted_iota(jnp.int32, xh.shape, 1)
    half = QK_ROPE // 2
    partner = jnp.where(lane < QK_NOPE + half,
                        pltpu.roll(xh, LANES - half, axis=1),
                        pltpu.roll(xh, half, axis=1))
    return xh * cos + partner * sin


def _inproj_kernel(x_ref, xh_ref, mod_ref, w1_ref, ws_ref, bs_ref, qn_ref, wuq_ref, kvn_ref,
                   wk_ref, wv_ref, cos_ref, sin_ref, wpool_ref, pscale_ref,
                   ya_ref, yc_ref, q_ref, k_ref, v_ref, *, seq, tb):
    i = pl.program_id(0)
    m = mod_ref[0]
    sh1, sc1 = m[0:1], m[1:2]
    h = (_rms(x_ref[...]) * (1.0 + sc1) + sh1).astype(BF16)
    z = jnp.dot(h, w1_ref[...], preferred_element_type=F32)

    gu = jax.nn.gelu(z[:, _C_ZU:_C_ZV])
    gv = jax.nn.gelu(z[:, _C_ZV:_C_CQ])
    mu = jnp.mean(gv, axis=-1, keepdims=True)
    dv = gv - mu
    vn = (dv * lax.rsqrt(jnp.mean(dv * dv, axis=-1, keepdims=True) + EPS)).astype(BF16)
    r = lax.broadcasted_iota(jnp.int32, (A_CHUNK, A_CHUNK), 0)
    cc = lax.broadcasted_iota(jnp.int32, (A_CHUNK, A_CHUNK), 1)
    lane_a = lax.broadcasted_iota(jnp.int32, (A_CHUNK, A_WIDTH), 1)
    wtril = [jnp.where(r >= cc, ws_ref[g], 0.0).astype(BF16) for g in range(A_GROUPS)]
    for ci in range(tb // A_CHUNK):
        lo = ci * A_CHUNK
        vc = vn[lo:lo + A_CHUNK]
        mixed = bs_ref[...]
        for g in range(A_GROUPS):
            dg = jnp.dot(wtril[g], vc, preferred_element_type=F32)
            in_g = (lane_a >= g * A_GROUP_DIM) & (lane_a < (g + 1) * A_GROUP_DIM)
            mixed = mixed + jnp.where(in_g, dg, 0.0)
        ya_ref[lo:lo + A_CHUNK, :] = (gu[lo:lo + A_CHUNK] * mixed).astype(BF16)

    zp = z[:, _C_ZP:_C_KR]
    t0 = (i * tb) % seq
    hh = (_rms(xh_ref[...]) * (1.0 + sc1) + sh1).astype(BF16)
    zph = jnp.dot(hh, w1_ref[:, _C_ZP:_C_KR], preferred_element_type=F32)
    zph = zph * jnp.where(t0 == 0, 0.0, 1.0)
    zext = jnp.concatenate([zph, zp], axis=0)
    n_ext = HALO + tb
    s2 = zext + pltpu.roll(zext, 1, axis=0)
    s4 = s2 + pltpu.roll(s2, 2, axis=0)
    s8 = s4 + pltpu.roll(s4, 4, axis=0)
    s16 = s8 + pltpu.roll(s8, 8, axis=0)
    lane_c = lax.broadcasted_iota(jnp.int32, (n_ext, C_WIDTH), 1)
    grp = lane_c // C_GROUP
    win = jnp.where(grp == 0, s2, jnp.where(grp == 1, s4, jnp.where(grp == 2, s8, s16)))
    wsize = jnp.where(grp == 0, 2, jnp.where(grp == 1, 4, jnp.where(grp == 2, 8, 16)))
    tpos = t0 - HALO + lax.broadcasted_iota(jnp.int32, (n_ext, C_WIDTH), 0)
    cnt = jnp.minimum(tpos + 1, wsize).astype(F32)
    pooled = (win / jnp.maximum(cnt, 1.0) - zext)[HALO:]
    yc = jnp.dot(pooled.astype(BF16), wpool_ref[...], preferred_element_type=F32)
    yc_ref[...] = (yc * pscale_ref[...]).astype(BF16)

    cos = cos_ref[...]
    sin = sin_ref[...]
    cqn = (_rms(z[:, _C_CQ:_C_CKV]) * qn_ref[...]).astype(BF16)
    q = jnp.dot(cqn, wuq_ref[...], preferred_element_type=F32)
    ckvn = (_rms(z[:, _C_CKV:_C_ZP]) * kvn_ref[...]).astype(BF16)
    kn = jnp.dot(ckvn, wk_ref[...], preferred_element_type=F32)
    vv = jnp.dot(ckvn, wv_ref[...], preferred_element_type=F32)
    lane_v = lax.broadcasted_iota(jnp.int32, vv.shape, 1)
    v_ref[...] = jnp.where(lane_v % HEAD_PAD == ONES_LANE, 1.0, vv).astype(BF16)
    krope = _rope_apply(z[:, _C_KR:_C_END], cos, sin)
    qscale = math.log2(math.e) / math.sqrt(QK_DIM)
    for hd in range(MLA_HEADS):
        sl = slice(hd * HEAD_PAD, (hd + 1) * HEAD_PAD)
        q_ref[:, sl] = (_rope_apply(q[:, sl], cos, sin) * qscale).astype(BF16)
        k_ref[:, sl] = (kn[:, sl] + krope).astype(BF16)


def _inproj(x, mod, w1, w_s, bs_full, q_norm, wuq, kv_norm, wk, wv, cos, sin, wpool, pscale, *, seq, tb):
    T, D = x.shape
    hpb = tb // HALO
    kern = functools.partial(_inproj_kernel, seq=seq, tb=tb)
    full = lambda a: pl.BlockSpec(a.shape, lambda i: (0,) * a.ndim)
    hw = MLA_HEADS * HEAD_PAD
    return pl.pallas_call(
        kern,
        grid=(T // tb,),
        in_specs=[
            pl.BlockSpec((tb, D), lambda i: (i, 0)),
            pl.BlockSpec((HALO, D), lambda i: (jnp.maximum(i * hpb - 1, 0), 0)),
            pl.BlockSpec((1, SUBLANES, D), lambda i: ((i * tb) // seq, 0, 0)),
            full(w1), full(w_s), full(bs_full), full(q_norm), full(wuq), full(kv_norm),
            full(wk), full(wv),
            pl.BlockSpec((tb, LANES), lambda i: (i, 0)),
            pl.BlockSpec((tb, LANES), lambda i: (i, 0)),
            full(wpool), full(pscale),
        ],
        out_specs=[
            pl.BlockSpec((tb, A_WIDTH), lambda i: (i, 0)),
            pl.BlockSpec((tb, C_WIDTH), lambda i: (i, 0)),
            pl.BlockSpec((tb, hw), lambda i: (i, 0)),
            pl.BlockSpec((tb, hw), lambda i: (i, 0)),
            pl.BlockSpec((tb, hw), lambda i: (i, 0)),
        ],
        out_shape=[
            jax.ShapeDtypeStruct((T, A_WIDTH), BF16),
            jax.ShapeDtypeStruct((T, C_WIDTH), BF16),
            jax.ShapeDtypeStruct((T, hw), BF16),
            jax.ShapeDtypeStruct((T, hw), BF16),
            jax.ShapeDtypeStruct((T, hw), BF16),
        ],
        compiler_params=_cparams(("arbitrary",)),
        name="inproj",
    )(x, x, mod, w1, w_s, bs_full, q_norm, wuq, kv_norm, wk, wv, cos, sin, wpool, pscale)


def _attn_kernel(q_ref, kt_ref, v_ref, o_ref, m_scr, acc_scr, *, ta):
    qi = pl.program_id(2)
    ki = pl.program_id(3)

    @pl.when(ki == 0)
    def _():
        m_scr[...] = jnp.full(m_scr.shape, NEG, F32)
        acc_scr[...] = jnp.zeros(acc_scr.shape, F32)

    def block(diagonal):
        n_chunks = ATT_CHUNKS if ta % (ATT_CHUNKS * SUBLANES) == 0 else 1
        rows = ta // n_chunks
        sls = [slice(h * rows, (h + 1) * rows) for h in range(n_chunks)]
        m_prevs = [m_scr[sl, :] for sl in sls]
        acc_prevs = [acc_scr[sl, :] for sl in sls]

        def n_cols(h):
            return (h + 1) * rows if diagonal else ta

        def scores(h):
            s = jnp.dot(q_ref[sls[h], :], kt_ref[:, :n_cols(h)], preferred_element_type=F32)
            if diagonal:
                row = h * rows + lax.broadcasted_iota(jnp.int32, s.shape, 0)
                col = lax.broadcasted_iota(jnp.int32, s.shape, 1)
                s = jnp.where(col <= row, s, NEG)
            return s

        def softmax(h, s):
            m_new = jnp.maximum(m_prevs[h], jnp.max(s, axis=1, keepdims=True))
            alpha = jnp.exp2(m_prevs[h] - m_new)
            return m_new, alpha, jnp.exp2(s - m_new[:, 0:1]).astype(BF16)

        def weighted(h, alpha, p):
            return alpha * acc_prevs[h] + jnp.dot(p, v_ref[:n_cols(h), :], preferred_element_type=F32)

        s_next = scores(0)
        pending = None
        done = []
        for h in range(n_chunks + 1):
            if h < n_chunks:
                s_cur = s_next
                if h + 1 < n_chunks:
                    s_next = scores(h + 1)
                m_new, alpha, p = softmax(h, s_cur)
            if pending is not None:
                hp, m_p, alpha_p, p_p = pending
                done.append((hp, m_p, weighted(hp, alpha_p, p_p)))
            pending = (h, m_new, alpha, p)
        for hp, m_p, acc in done:
            m_scr[sls[hp], :] = m_p
            acc_scr[sls[hp], :] = acc

    @pl.when(ki < qi)
    def _():
        block(False)

    @pl.when(ki == qi)
    def _():
        block(True)
        acc = acc_scr[...]
        o_ref[...] = (acc / acc[:, ONES_LANE:ONES_LANE + 1]).astype(BF16)


def _attention(q, kt, v, *, batch, seq, ta):
    T, hw = q.shape
    nq = seq // ta
    kern = functools.partial(_attn_kernel, ta=ta)
    qspec = pl.BlockSpec((ta, HEAD_PAD), lambda b, h, qi, ki: (b * nq + qi, h))
    kspec = pl.BlockSpec((ta, HEAD_PAD), lambda b, h, qi, ki: (b * nq + jnp.minimum(ki, qi), h))
    ktspec = pl.BlockSpec((HEAD_PAD, ta), lambda b, h, qi, ki: (h, b * nq + jnp.minimum(ki, qi)))
    return pl.pallas_call(
        kern,
        grid=(batch, MLA_HEADS, nq, nq),
        in_specs=[qspec, ktspec, kspec],
        out_specs=qspec,
        out_shape=jax.ShapeDtypeStruct((T, hw), BF16),
        scratch_shapes=[pltpu.VMEM((ta, HEAD_PAD), F32)] * 2,
        compiler_params=_cparams(("arbitrary",) * 4),
        name="attention",
    )(q, kt, v)


def _merge_kernel(x_ref, mod_ref, ya_ref, yb_ref, yc_ref, wg_ref, wa_ref, wb_ref, wc_ref, wo_ref,
                  xo_ref):
    D = D_MODEL
    x = x_ref[...]
    m = mod_ref[0]
    sh1, sc1, g1 = m[0:1], m[1:2], m[2:3]
    h = (_rms(x) * (1.0 + sc1) + sh1).astype(BF16)
    gates = jax.nn.sigmoid(jnp.dot(h, wg_ref[...], preferred_element_type=F32))
    merged = gates[:, 0:D] * jnp.dot(ya_ref[...], wa_ref[...], preferred_element_type=F32)
    merged += gates[:, D:2 * D] * jnp.dot(yb_ref[...], wb_ref[...], preferred_element_type=F32)
    merged += gates[:, 2 * D:3 * D] * jnp.dot(yc_ref[...], wc_ref[...], preferred_element_type=F32)
    xo_ref[...] = x + g1 * jnp.dot(merged.astype(BF16), wo_ref[...], preferred_element_type=F32)


def _merge(x, mod, ya, yb, yc, wg, wa, wb, wc, wo, *, seq, tb):
    T, D = x.shape
    full = lambda a: pl.BlockSpec(a.shape, lambda i: (0,) * a.ndim)
    row = lambda w: pl.BlockSpec((tb, w), lambda i: (i, 0))
    return pl.pallas_call(
        _merge_kernel,
        grid=(T // tb,),
        in_specs=[row(D), pl.BlockSpec((1, SUBLANES, D), lambda i: ((i * tb) // seq, 0, 0)),
                  row(ya.shape[1]), row(yb.shape[1]), row(yc.shape[1]),
                  full(wg), full(wa), full(wb), full(wc), full(wo)],
        out_specs=row(D),
        out_shape=jax.ShapeDtypeStruct((T, D), F32),
        compiler_params=_cparams(("arbitrary",)),
        name="merge",
    )(x, mod, ya, yb, yc, wg, wa, wb, wc, wo)


def _top16_keys(s, vals_ref, idx_ref):
    half = s.shape[0] // 2
    a, b = s[:half], s[half:]
    rows = lax.broadcasted_iota(jnp.int32, a.shape, 0).astype(F32)
    a_first = a >= b
    act, act_id = jnp.where(a_first, a, b), jnp.where(a_first, rows, rows + half)
    rest, rest_id = jnp.where(a_first, b, a), jnp.where(a_first, rows + half, rows)
    for j in range(PEER_TOPK):
        mx = jnp.max(act, axis=0, keepdims=True)
        pos = jnp.min(jnp.where(act == mx, act_id, float(s.shape[0])), axis=0, keepdims=True)
        hit = act_id == pos
        vals_ref[j:j + 1, :] = mx
        idx_ref[j:j + 1, :] = pos
        act, act_id = jnp.where(hit, rest, act), jnp.where(hit, rest_id, act_id)
        rest = jnp.where(hit, NEG, rest)


def _top16_payload(s, payload, vals_ref, idx_ref):
    rows = lax.broadcasted_iota(jnp.int32, s.shape, 0).astype(F32)
    for j in range(PEER_TOPK):
        mx = jnp.max(s, axis=0, keepdims=True)
        pos = jnp.min(jnp.where(s == mx, rows, float(s.shape[0])), axis=0, keepdims=True)
        hit = rows == pos
        vals_ref[j:j + 1, :] = mx
        picked = jnp.sum(jnp.where(hit, payload, 0.0), axis=0, keepdims=True)
        idx_ref[j:j + 1, :] = picked.astype(jnp.int32)
        s = jnp.where(hit, NEG, s)


def _route_kernel(x_ref, mod_ref, wpq_ref, keys_ref, h2_ref, idx_ref, g_ref,
                  h2_scr, v1_scr, i1_scr, v2_scr, i2_scr, tv_scr):
    hd = pl.program_id(1)

    @pl.when(hd == 0)
    def _():
        m = mod_ref[0]
        sh2, sc2 = m[3:4], m[4:5]
        h2 = _rms(x_ref[...]) * (1.0 + sc2) + sh2
        h2_ref[...] = h2
        h2_scr[...] = h2.astype(BF16)

    q = jnp.dot(h2_scr[...], wpq_ref[...], preferred_element_type=F32).astype(BF16)
    nt = (((1,), (1,)), ((), ()))
    s1 = lax.dot_general(keys_ref[0, 0], q[:, :PEER_HALF], nt, preferred_element_type=F32)
    s2 = lax.dot_general(keys_ref[0, 1], q[:, PEER_HALF:], nt, preferred_element_type=F32)
    _top16_keys(s1, v1_scr, i1_scr)
    _top16_keys(s2, v2_scr, i2_scr)
    v1, v2 = v1_scr[...], v2_scr[...]
    i1, i2 = i1_scr[...] * N_KEYS, i2_scr[...]

    r8 = lax.broadcasted_iota(jnp.int32, (SUBLANES, v1.shape[1]), 0)
    low = r8 < 4

    def two(a0, a1, nb0, nb1, val, idx):
        va = jnp.where(low, val[a0:a0 + 1], val[a1:a1 + 1])
        ia = jnp.where(low, idx[a0:a0 + 1], idx[a1:a1 + 1])
        ok = r8 < jnp.where(low, nb0, nb1 + 4)
        return va, ia, ok

    v2lo, i2lo = v2[0:SUBLANES], i2[0:SUBLANES]
    v2rep = jnp.where(low, v2lo, pltpu.roll(v2lo, 4, axis=0))
    i2rep = jnp.where(low, i2lo, pltpu.roll(i2lo, 4, axis=0))
    cand = [v1[0:1] + v2[0:SUBLANES], v1[0:1] + v2[SUBLANES:], v1[1:2] + v2lo]
    cidx = [i1[0:1] + i2[0:SUBLANES], i1[0:1] + i2[SUBLANES:], i1[1:2] + i2lo]
    for a, nb in ((2, 5), (3, 4)):
        cand.append(jnp.where(r8 < nb, v1[a:a + 1] + v2lo, NEG))
        cidx.append(i1[a:a + 1] + i2lo)
    for a0, a1, nb0, nb1 in ((4, 5, 3, 2), (6, 7, 2, 2)):
        va, ia, ok = two(a0, a1, nb0, nb1, v1, i1)
        cand.append(jnp.where(ok, va + v2rep, NEG))
        cidx.append(ia + i2rep)
    cand.append(v1[SUBLANES:] + v2[0:1])
    cidx.append(i1[SUBLANES:] + i2[0:1])
    cand = jnp.concatenate(cand, axis=0)
    cidx = jnp.concatenate(cidx, axis=0)
    _top16_payload(cand, cidx, tv_scr, idx_ref)
    tv = tv_scr[...]
    e = jnp.exp(tv - tv[0:1])
    g_ref[...] = e / jnp.sum(e, axis=0, keepdims=True)


def _route(x, mod, wpq, keys, *, seq, tb):
    T, D = x.shape
    kd = 2 * PEER_HALF
    return pl.pallas_call(
        _route_kernel,
        grid=(T // tb, PEER_HEADS),
        in_specs=[
            pl.BlockSpec((tb, D), lambda i, h: (i, 0)),
            pl.BlockSpec((1, SUBLANES, D), lambda i, h: ((i * tb) // seq, 0, 0)),
            pl.BlockSpec((D, kd), lambda i, h: (0, h)),
            pl.BlockSpec((1, 2, N_KEYS, PEER_HALF), lambda i, h: (h, 0, 0, 0)),
        ],
        out_specs=[
            pl.BlockSpec((tb, D), lambda i, h: (i, 0)),
            pl.BlockSpec((PEER_TOPK, tb), lambda i, h: (h, i)),
            pl.BlockSpec((PEER_TOPK, tb), lambda i, h: (h, i)),
        ],
        out_shape=[
            jax.ShapeDtypeStruct((T, D), F32),
            jax.ShapeDtypeStruct((N_PAIRS, T), jnp.int32),
            jax.ShapeDtypeStruct((N_PAIRS, T), F32),
        ],
        scratch_shapes=[
            pltpu.VMEM((tb, D), BF16),
            pltpu.VMEM((PEER_TOPK, tb), F32), pltpu.VMEM((PEER_TOPK, tb), F32),
            pltpu.VMEM((PEER_TOPK, tb), F32), pltpu.VMEM((PEER_TOPK, tb), F32),
            pltpu.VMEM((PEER_TOPK, tb), F32),
        ],
        compiler_params=_cparams(("arbitrary", "arbitrary")),
        name="route",
    )(x, mod, wpq, keys)


def _load_table(tab_hbm, tab_vmem, sem):
    @pl.when(pl.program_id(0) == 0)
    def _():
        cp = pltpu.make_async_copy(tab_hbm, tab_vmem, sem)
        cp.start()
        cp.wait()


HALF_TILE = SUBLANES // 2
HALF_PAIRS = N_PAIRS // 2
W_ROWS = HALF_PAIRS * SUBLANES


def _pair_tiles(tab_vmem, rows_ref, t):
    tiles = []
    for c in range(N_PAIRS // ROW_CHUNK):
        rows_c = rows_ref.at[pl.ds(t * N_PAIRS + c * ROW_CHUNK, ROW_CHUNK)]
        tiles += [tab_vmem[pl.ds(pl.multiple_of(rows_c[j], HALF_TILE), HALF_TILE), :] for j in range(ROW_CHUNK)]
    wa = pltpu.bitcast(jnp.concatenate(tiles[:HALF_PAIRS], axis=0), BF16)
    wb = pltpu.bitcast(jnp.concatenate(tiles[HALF_PAIRS:], axis=0), BF16)
    return jnp.concatenate([wa, wb], axis=1)


ROW_CHUNK = 16
TOKENS_PER_TRIP = 32


def _token_loop(tb, token):
    def trip(i, carry):
        for u in range(TOKENS_PER_TRIP):
            token(i * TOKENS_PER_TRIP + u, carry)
        return carry
    lax.fori_loop(0, tb // TOKENS_PER_TRIP, trip, 0)


def _expansion_constants():
    col = np.arange(2 * W_ROWS)
    pair = (col // W_ROWS) * HALF_PAIRS + (col % W_ROWS) // SUBLANES
    g = np.zeros((2 * W_ROWS, N_PAIRS), np.float32)
    g[col, pair] = 1.0
    return g, g.T.copy()


def _peer_u_kernel(rows_ref, h_ref, g_ref, gsum_ref, spread_ref, tab_hbm, cexp_ref,
                   tab_vmem, stage, sem, *, tb):
    _load_table(tab_hbm, tab_vmem, sem)
    sub = lax.broadcasted_iota(jnp.int32, (2 * SUBLANES, W_ROWS), 0)
    lane = lax.broadcasted_iota(jnp.int32, (2 * SUBLANES, W_ROWS), 1)
    diag = (lane & (SUBLANES - 1)) == (sub & (SUBLANES - 1))
    nt = (((1,), (1,)), ((), ()))

    def token(t, carry):
        w = _pair_tiles(tab_vmem, rows_ref, t)
        hrow = h_ref[pl.ds(t, 1), :]
        xt = jnp.concatenate([hrow[:, r * LANES:(r + 1) * LANES] for r in range(SUBLANES)], axis=0).astype(BF16)
        z = jnp.zeros_like(xt)
        x2 = jnp.concatenate([jnp.concatenate([xt, z], axis=1),
                              jnp.concatenate([z, xt], axis=1)], axis=0)
        res = lax.dot_general(x2, w, nt, preferred_element_type=F32)
        m = jnp.where(diag, res, 0.0)
        stage[pl.ds(t, 1), 0:W_ROWS] = jnp.sum(m[:SUBLANES], axis=0, keepdims=True)
        stage[pl.ds(t, 1), W_ROWS:] = jnp.sum(m[SUBLANES:], axis=0, keepdims=True)
        return carry

    _token_loop(tb, token)
    s = stage[...]
    hi = s.astype(BF16)
    lo = (s - hi.astype(F32)).astype(BF16)
    a = (jnp.dot(hi, gsum_ref[...], preferred_element_type=F32)
         + jnp.dot(lo, gsum_ref[...], preferred_element_type=F32))
    coef = (g_ref[...] * jax.nn.gelu(a)).astype(BF16)
    cexp_ref[...] = jnp.dot(coef, spread_ref[...], preferred_element_type=F32)


def _peer_u(rows_tm, h2, g_tm, gsum, spread, tab, *, tb):
    T = h2.shape[0]
    kern = functools.partial(_peer_u_kernel, tb=tb)
    full = lambda a: pl.BlockSpec(a.shape, lambda i: (0,) * a.ndim)
    return pl.pallas_call(
        kern,
        grid=(T // tb,),
        in_specs=[
            pl.BlockSpec((tb * N_PAIRS,), lambda i: (i,), memory_space=pltpu.SMEM),
            pl.BlockSpec((tb, D_MODEL), lambda i: (i, 0)),
            pl.BlockSpec((tb, N_PAIRS), lambda i: (i, 0)),
            full(gsum), full(spread),
            pl.BlockSpec(memory_space=pl.ANY),
        ],
        out_specs=pl.BlockSpec((tb, 2 * W_ROWS), lambda i: (i, 0)),
        out_shape=jax.ShapeDtypeStruct((T, 2 * W_ROWS), F32),
        scratch_shapes=[
            pltpu.VMEM((N_EXPERTS * HALF_TILE, LANES), jnp.uint32),
            pltpu.VMEM((tb, 2 * W_ROWS), F32),
            pltpu.SemaphoreType.DMA,
        ],
        compiler_params=_cparams(("arbitrary",), TABLE_VMEM_LIMIT),
        name="peer_u",
    )(rows_tm, h2, g_tm, gsum, spread, tab)


def _peer_v_kernel(rows_ref, cexp_ref, x_ref, g2_ref, tab_hbm, xo_ref, tab_vmem, sem, *, tb):
    _load_table(tab_hbm, tab_vmem, sem)
    g2 = g2_ref[0]
    sub = lax.broadcasted_iota(jnp.int32, (SUBLANES, LANES), 0)
    lane = lax.broadcasted_iota(jnp.int32, (SUBLANES, LANES), 1)
    diag = (lane & (SUBLANES - 1)) == sub
    n_chunks = W_ROWS // LANES

    def token(t, carry):
        w = _pair_tiles(tab_vmem, rows_ref, t)
        crow = cexp_ref[pl.ds(t, 1), :]
        blocks = []
        for blk in range(2):
            lo = [(blk * n_chunks + q) * LANES for q in range(n_chunks)]
            chunks = [jnp.where(diag, jnp.broadcast_to(crow[:, c:c + LANES], diag.shape), 0.0) for c in lo]
            blocks.append(jnp.concatenate(chunks, axis=1))
        c = jnp.concatenate(blocks, axis=0).astype(BF16)
        res = jnp.dot(c, w, preferred_element_type=F32)
        out = g2 * (res[:SUBLANES, :LANES] + res[SUBLANES:, LANES:])
        delta = jnp.concatenate([out[r:r + 1, :] for r in range(SUBLANES)], axis=1)
        xo_ref[pl.ds(t, 1), :] = x_ref[pl.ds(t, 1), :] + delta
        return carry

    _token_loop(tb, token)


def _peer_v(rows_tm, cexp, x, g2, tab, *, seq, tb):
    T = x.shape[0]
    kern = functools.partial(_peer_v_kernel, tb=tb)
    tok = pl.BlockSpec((tb, D_MODEL), lambda i: (i, 0))
    return pl.pallas_call(
        kern,
        grid=(T // tb,),
        in_specs=[
            pl.BlockSpec((tb * N_PAIRS,), lambda i: (i,), memory_space=pltpu.SMEM),
            pl.BlockSpec((tb, 2 * W_ROWS), lambda i: (i, 0)),
            tok,
            pl.BlockSpec((1, SUBLANES, LANES), lambda i: ((i * tb) // seq, 0, 0)),
            pl.BlockSpec(memory_space=pl.ANY),
        ],
        out_specs=tok,
        out_shape=jax.ShapeDtypeStruct(x.shape, F32),
        scratch_shapes=[
            pltpu.VMEM((N_EXPERTS * HALF_TILE, LANES), jnp.uint32),
            pltpu.SemaphoreType.DMA,
        ],
        compiler_params=_cparams(("arbitrary",), TABLE_VMEM_LIMIT),
        name="peer_v",
    )(rows_tm, cexp, x, g2, tab)


def _final_kernel(x_ref, g_ref, o_ref):
    o_ref[...] = _rms(x_ref[...]) * g_ref[...]


def _final_norm(x, gain, tb):
    T, D = x.shape
    return pl.pallas_call(
        _final_kernel,
        grid=(T // tb,),
        in_specs=[pl.BlockSpec((tb, D), lambda i: (i, 0)), pl.BlockSpec((1, D), lambda i: (0, 0))],
        out_specs=pl.BlockSpec((tb, D), lambda i: (i, 0)),
        out_shape=jax.ShapeDtypeStruct((T, D), F32),
        compiler_params=_cparams(("arbitrary",)),
        name="final_norm",
    )(x, gain)


def _pack_table(tab):
    lead = tab.shape[:-2]
    pairs = tab.astype(BF16).reshape(*lead, N_EXPERTS, HALF_TILE, 2, LANES)
    packed = lax.bitcast_convert_type(jnp.swapaxes(pairs, -1, -2), jnp.uint32)
    return packed.reshape(*lead, N_EXPERTS * HALF_TILE, LANES)


def _pad_heads(w, per_head, offset):
    k = w.shape[0]
    w = w.reshape(k, MLA_HEADS, per_head)
    out = jnp.zeros((k, MLA_HEADS, HEAD_PAD), w.dtype)
    out = out.at[:, :, offset:offset + per_head].set(w)
    return out.reshape(k, MLA_HEADS * HEAD_PAD)


def _block_size(seq, want):
    tb = min(want, seq)
    assert seq % tb == 0
    return tb


def _layer(xf, mod_l, cos, sin, w_in, w_s, b_s, q_norm, w_uq, kv_norm, w_ukv, w_pool, pool_scale,
           w_a, w_b, w_c, w_o, w_pq, sub_keys, u_packed, v_packed, *, batch, seq):
    T, D = xf.shape
    tb_in = _block_size(seq, 512)
    tb_mg = _block_size(seq, 256)
    tb_rt = _block_size(seq, 1024)
    ta = _block_size(seq, 2048)
    tb_peer = LANES

    sp = (0, 256, 512, 896, 1152, 1184, 1440)
    kr_pad = jnp.zeros((D, LANES), F32).at[:, QK_NOPE:QK_NOPE + QK_ROPE].set(w_in[:, sp[4]:sp[5]])
    w1 = jnp.concatenate([w_in[:, :sp[4]], w_in[:, sp[5]:sp[6]], kr_pad], axis=1).astype(BF16)
    wg = w_in[:, sp[6]:].astype(BF16)
    bs_full = jnp.repeat(b_s.T, A_GROUP_DIM, axis=1)
    wuq = _pad_heads(w_uq, QK_DIM, 0).astype(BF16)
    wukv = w_ukv.reshape(KV_LORA, MLA_HEADS, QK_NOPE + V_DIM)
    wk = _pad_heads(wukv[:, :, :QK_NOPE].reshape(KV_LORA, -1), QK_NOPE, 0).astype(BF16)
    wv = _pad_heads(wukv[:, :, QK_NOPE:].reshape(KV_LORA, -1), V_DIM, 0).astype(BF16)
    wpool = jax.scipy.linalg.block_diag(*[w_pool[g] for g in range(len(POOL_WINDOWS))]).astype(BF16)
    wb = w_b.reshape(MLA_HEADS, V_DIM, D)
    wb = jnp.pad(wb, ((0, 0), (0, HEAD_PAD - V_DIM), (0, 0))).reshape(MLA_HEADS * HEAD_PAD, D).astype(BF16)

    ya, yc, q, k, v = _inproj(
        xf, mod_l, w1, w_s, bs_full, q_norm.reshape(1, -1), wuq, kv_norm.reshape(1, -1),
        wk, wv, cos, sin, wpool, pool_scale.reshape(1, -1), seq=seq, tb=tb_in)
    yb = _attention(q, k.T, v, batch=batch, seq=seq, ta=ta)
    x1 = _merge(xf, mod_l, ya, yb, yc, wg, w_a.astype(BF16), wb, w_c.astype(BF16),
                w_o.astype(BF16), seq=seq, tb=tb_mg)
    h2, idx_t, g_t = _route(x1, mod_l, w_pq.astype(BF16), sub_keys.astype(BF16), seq=seq, tb=tb_rt)
    rows_tm = (idx_t.T * HALF_TILE).reshape(-1)
    gsum, spread = (jnp.asarray(a, BF16) for a in _expansion_constants())
    cexp = _peer_u(rows_tm, h2, g_t.T, gsum, spread,
                   u_packed, tb=tb_peer)
    g2 = mod_l[:, 5].reshape(batch, SUBLANES, LANES)
    x2 = _peer_v(rows_tm, cexp, x1, g2, v_packed, seq=seq, tb=tb_peer)
    return x2, (x1, h2)


def kernel(x, c, positions, w_mod, b_mod, w_in, w_s, b_s, q_norm, w_uq, kv_norm, w_ukv, w_pool, pool_scale, w_a, w_b, w_c, w_o, w_pq, sub_keys, u_tab, v_tab, final_norm):
    B, S, D = x.shape
    L = w_mod.shape[0]
    T = B * S
    assert D == D_MODEL and S % A_CHUNK == 0

    c_pad = jnp.zeros((SUBLANES, D), F32).at[:B].set(c)
    mod = _modulation(c_pad, w_mod, b_mod)[:, :B].reshape(L, B, 6, D)
    mod = jnp.pad(mod, ((0, 0), (0, 0), (0, SUBLANES - 6), (0, 0)))

    inv_freq = 1.0 / (ROPE_THETA ** (jnp.arange(0, QK_ROPE, 2, dtype=F32) / QK_ROPE))
    half = QK_ROPE // 2
    freq_lane = jnp.zeros((LANES,), F32)
    freq_lane = freq_lane.at[QK_NOPE:QK_NOPE + half].set(inv_freq)
    freq_lane = freq_lane.at[QK_NOPE + half:QK_NOPE + QK_ROPE].set(inv_freq)
    ang = positions.astype(F32).reshape(T, 1) * freq_lane[None, :]
    cos, sin = _rope_tables(ang, _block_size(T, 1024))

    u_packed = _pack_table(u_tab)
    v_packed = _pack_table(v_tab)
    xf = x.reshape(T, D)
    for l in range(L):
        xf, _ = _layer(xf, mod[l], cos, sin, w_in[l], w_s[l], b_s[l], q_norm[l], w_uq[l], kv_norm[l],
                       w_ukv[l], w_pool[l], pool_scale[l], w_a[l], w_b[l], w_c[l], w_o[l], w_pq[l],
                       sub_keys[l], u_packed[l], v_packed[l], batch=B, seq=S)

    out = _final_norm(xf, final_norm.reshape(1, D), _block_size(T, 1024))
    return out.reshape(B, S, D)
```

```python
import functools
import math

import numpy as np
import jax
import jax.numpy as jnp
from jax import lax
from jax.experimental import pallas as pl
from jax.experimental.pallas import tpu as pltpu

F32 = jnp.float32
BF16 = jnp.bfloat16

EPS = 1e-6
D_MODEL = 1024
A_WIDTH = 256
A_GROUPS = 4
A_GROUP_DIM = A_WIDTH // A_GROUPS
A_CHUNK = 128
MLA_HEADS = 8
Q_LORA = 384
KV_LORA = 256
QK_NOPE = 64
QK_ROPE = 32
V_DIM = 64
QK_DIM = QK_NOPE + QK_ROPE
ROPE_THETA = 10000.0
POOL_WINDOWS = (2, 4, 8, 16)
C_WIDTH = 256
C_GROUP = C_WIDTH // len(POOL_WINDOWS)
PEER_HEADS = 8
N_KEYS = 128
N_EXPERTS = N_KEYS * N_KEYS
PEER_HALF = 128
PEER_TOPK = 16
N_PAIRS = PEER_HEADS * PEER_TOPK

LANES = 128
SUBLANES = 8
HEAD_PAD = LANES
HALO = 16
NEG = -1e30
ATT_CHUNKS = 8
ONES_LANE = V_DIM

VMEM_LIMIT = 48 * 1024 * 1024
TABLE_VMEM_LIMIT = 56 * 1024 * 1024

_C_ZU, _C_ZV, _C_CQ, _C_CKV, _C_ZP, _C_KR, _C_END = 0, 256, 512, 896, 1152, 1408, 1536


def _rms(x):
    return x * lax.rsqrt(jnp.mean(x * x, axis=-1, keepdims=True) + EPS)


def _cparams(sem, limit=VMEM_LIMIT):
    return pltpu.CompilerParams(dimension_semantics=sem, vmem_limit_bytes=limit)


def _mod_kernel(c_ref, w_ref, b_ref, o_ref):
    c = c_ref[...]
    ca = c * jax.nn.sigmoid(c)
    o_ref[0] = jnp.dot(ca, w_ref[0], preferred_element_type=F32) + b_ref[0]


def _modulation(c_pad, w_mod, b_mod):
    L, D, six_d = w_mod.shape
    nb = six_d // D
    rows = c_pad.shape[0]
    return pl.pallas_call(
        _mod_kernel,
        grid=(L, nb),
        in_specs=[
            pl.BlockSpec((rows, D), lambda l, j: (0, 0)),
            pl.BlockSpec((1, D, D), lambda l, j: (l, 0, j)),
            pl.BlockSpec((1, 1, D), lambda l, j: (l, 0, j)),
        ],
        out_specs=pl.BlockSpec((1, rows, D), lambda l, j: (l, 0, j)),
        out_shape=jax.ShapeDtypeStruct((L, rows, six_d), F32),
        compiler_params=_cparams(("arbitrary", "arbitrary")),
        name="modulation",
    )(c_pad, w_mod, b_mod.reshape(L, 1, six_d))


def _rope_kernel(ang_ref, cos_ref, sin_ref):
    ang = ang_ref[...]
    lane = lax.broadcasted_iota(jnp.int32, ang.shape, 1)
    first_half = lane < QK_NOPE + QK_ROPE // 2
    cos_ref[...] = jnp.cos(ang)
    s = jnp.sin(ang)
    sin_ref[...] = jnp.where(first_half, -s, s)


def _rope_tables(ang, tb):
    T = ang.shape[0]
    spec = pl.BlockSpec((tb, LANES), lambda i: (i, 0))
    return pl.pallas_call(
        _rope_kernel,
        grid=(T // tb,),
        in_specs=[spec],
        out_specs=[spec, spec],
        out_shape=[jax.ShapeDtypeStruct((T, LANES), F32)] * 2,
        compiler_params=_cparams(("arbitrary",)),
        name="rope_tables",
    )(ang)


def _rope_apply(xh, cos, sin):
    lane = lax.broadcasted_iota(jnp.int32, xh.shape, 1)
    half = QK_ROPE // 2
    partner = jnp.where(lane < QK_NOPE + half,
                        pltpu.roll(xh, LANES - half, axis=1),
                        pltpu.roll(xh, half, axis=1))
    return xh * cos + partner * sin


def _inproj_kernel(x_ref, xh_ref, mod_ref, w1_ref, ws_ref, bs_ref, qn_ref, wuq_ref, kvn_ref,
                   wk_ref, wv_ref, cos_ref, sin_ref, wpool_ref, pscale_ref,
                   ya_ref, yc_ref, q_ref, k_ref, v_ref, *, seq, tb):
    i = pl.program_id(0)
    m = mod_ref[0]
    sh1, sc1 = m[0:1], m[1:2]
    h = (_rms(x_ref[...]) * (1.0 + sc1) + sh1).astype(BF16)
    z = jnp.dot(h, w1_ref[...], preferred_element_type=F32)

    gu = jax.nn.gelu(z[:, _C_ZU:_C_ZV])
    gv = jax.nn.gelu(z[:, _C_ZV:_C_CQ])
    mu = jnp.mean(gv, axis=-1, keepdims=True)
    dv = gv - mu
    vn = (dv * lax.rsqrt(jnp.mean(dv * dv, axis=-1, keepdims=True) + EPS)).astype(BF16)
    r = lax.broadcasted_iota(jnp.int32, (A_CHUNK, A_CHUNK), 0)
    cc = lax.broadcasted_iota(jnp.int32, (A_CHUNK, A_CHUNK), 1)
    lane_a = lax.broadcasted_iota(jnp.int32, (A_CHUNK, A_WIDTH), 1)
    wtril = [jnp.where(r >= cc, ws_ref[g], 0.0).astype(BF16) for g in range(A_GROUPS)]
    for ci in range(tb // A_CHUNK):
        lo = ci * A_CHUNK
        vc = vn[lo:lo + A_CHUNK]
        mixed = bs_ref[...]
        for g in range(A_GROUPS):
            dg = jnp.dot(wtril[g], vc, preferred_element_type=F32)
            in_g = (lane_a >= g * A_GROUP_DIM) & (lane_a < (g + 1) * A_GROUP_DIM)
            mixed = mixed + jnp.where(in_g, dg, 0.0)
        ya_ref[lo:lo + A_CHUNK, :] = (gu[lo:lo + A_CHUNK] * mixed).astype(BF16)

    zp = z[:, _C_ZP:_C_KR]
    t0 = (i * tb) % seq
    hh = (_rms(xh_ref[...]) * (1.0 + sc1) + sh1).astype(BF16)
    zph = jnp.dot(hh, w1_ref[:, _C_ZP:_C_KR], preferred_element_type=F32)
    zph = zph * jnp.where(t0 == 0, 0.0, 1.0)
    zext = jnp.concatenate([zph, zp], axis=0)
    n_ext = HALO + tb
    s2 = zext + pltpu.roll(zext, 1, axis=0)
    s4 = s2 + pltpu.roll(s2, 2, axis=0)
    s8 = s4 + pltpu.roll(s4, 4, axis=0)
    s16 = s8 + pltpu.roll(s8, 8, axis=0)
    lane_c = lax.broadcasted_iota(jnp.int32, (n_ext, C_WIDTH), 1)
    grp = lane_c // C_GROUP
    win = jnp.where(grp == 0, s2, jnp.where(grp == 1, s4, jnp.where(grp == 2, s8, s16)))
    wsize = jnp.where(grp == 0, 2, jnp.where(grp == 1, 4, jnp.where(grp == 2, 8, 16)))
    tpos = t0 - HALO + lax.broadcasted_iota(jnp.int32, (n_ext, C_WIDTH), 0)
    cnt = jnp.minimum(tpos + 1, wsize).astype(F32)
    pooled = (win / jnp.maximum(cnt, 1.0) - zext)[HALO:]
    yc = jnp.dot(pooled.astype(BF16), wpool_ref[...], preferred_element_type=F32)
    yc_ref[...] = (yc * pscale_ref[...]).astype(BF16)

    cos = cos_ref[...]
    sin = sin_ref[...]
    cqn = (_rms(z[:, _C_CQ:_C_CKV]) * qn_ref[...]).astype(BF16)
    q = jnp.dot(cqn, wuq_ref[...], preferred_element_type=F32)
    ckvn = (_rms(z[:, _C_CKV:_C_ZP]) * kvn_ref[...]).astype(BF16)
    kn = jnp.dot(ckvn, wk_ref[...], preferred_element_type=F32)
    vv = jnp.dot(ckvn, wv_ref[...], preferred_element_type=F32)
    lane_v = lax.broadcasted_iota(jnp.int32, vv.shape, 1)
    v_ref[...] = jnp.where(lane_v % HEAD_PAD == ONES_LANE, 1.0, vv).astype(BF16)
    krope = _rope_apply(z[:, _C_KR:_C_END], cos, sin)
    qscale = math.log2(math.e) / math.sqrt(QK_DIM)
    for hd in range(MLA_HEADS):
        sl = slice(hd * HEAD_PAD, (hd + 1) * HEAD_PAD)
        q_ref[:, sl] = (_rope_apply(q[:, sl], cos, sin) * qscale).astype(BF16)
        k_ref[:, sl] = (kn[:, sl] + krope).astype(BF16)


def _inproj(x, mod, w1, w_s, bs_full, q_norm, wuq, kv_norm, wk, wv, cos, sin, wpool, pscale, *, seq, tb):
    T, D = x.shape
    hpb = tb // HALO
    kern = functools.partial(_inproj_kernel, seq=seq, tb=tb)
    full = lambda a: pl.BlockSpec(a.shape, lambda i: (0,) * a.ndim)
    hw = MLA_HEADS * HEAD_PAD
    return pl.pallas_call(
        kern,
        grid=(T // tb,),
        in_specs=[
            pl.BlockSpec((tb, D), lambda i: (i, 0)),
            pl.BlockSpec((HALO, D), lambda i: (jnp.maximum(i * hpb - 1, 0), 0)),
            pl.BlockSpec((1, SUBLANES, D), lambda i: ((i * tb) // seq, 0, 0)),
            full(w1), full(w_s), full(bs_full), full(q_norm), full(wuq), full(kv_norm),
            full(wk), full(wv),
            pl.BlockSpec((tb, LANES), lambda i: (i, 0)),
            pl.BlockSpec((tb, LANES), lambda i: (i, 0)),
            full(wpool), full(pscale),
        ],
        out_specs=[
            pl.BlockSpec((tb, A_WIDTH), lambda i: (i, 0)),
            pl.BlockSpec((tb, C_WIDTH), lambda i: (i, 0)),
            pl.BlockSpec((tb, hw), lambda i: (i, 0)),
            pl.BlockSpec((tb, hw), lambda i: (i, 0)),
            pl.BlockSpec((tb, hw), lambda i: (i, 0)),
        ],
        out_shape=[
            jax.ShapeDtypeStruct((T, A_WIDTH), BF16),
            jax.ShapeDtypeStruct((T, C_WIDTH), BF16),
            jax.ShapeDtypeStruct((T, hw), BF16),
            jax.ShapeDtypeStruct((T, hw), BF16),
            jax.ShapeDtypeStruct((T, hw), BF16),
        ],
        compiler_params=_cparams(("arbitrary",)),
        name="inproj",
    )(x, x, mod, w1, w_s, bs_full, q_norm, wuq, kv_norm, wk, wv, cos, sin, wpool, pscale)


def _attn_kernel(q_ref, kt_ref, v_ref, o_ref, m_scr, acc_scr, *, ta):
    qi = pl.program_id(2)
    ki = pl.program_id(3)

    @pl.when(ki == 0)
    def _():
        m_scr[...] = jnp.full(m_scr.shape, NEG, F32)
        acc_scr[...] = jnp.zeros(acc_scr.shape, F32)

    def block(diagonal):
        n_chunks = ATT_CHUNKS if ta % (ATT_CHUNKS * SUBLANES) == 0 else 1
        rows = ta // n_chunks
        sls = [slice(h * rows, (h + 1) * rows) for h in range(n_chunks)]
        m_prevs = [m_scr[sl, :] for sl in sls]
        acc_prevs = [acc_scr[sl, :] for sl in sls]

        def n_cols(h):
            return (h + 1) * rows if diagonal else ta

        def scores(h):
            s = jnp.dot(q_ref[sls[h], :], kt_ref[:, :n_cols(h)], preferred_element_type=F32)
            if diagonal:
                row = h * rows + lax.broadcasted_iota(jnp.int32, s.shape, 0)
                col = lax.broadcasted_iota(jnp.int32, s.shape, 1)
                s = jnp.where(col <= row, s, NEG)
            return s

        def softmax(h, s):
            m_new = jnp.maximum(m_prevs[h], jnp.max(s, axis=1, keepdims=True))
            alpha = jnp.exp2(m_prevs[h] - m_new)
            return m_new, alpha, jnp.exp2(s - m_new[:, 0:1]).astype(BF16)

        def weighted(h, alpha, p):
            return alpha * acc_prevs[h] + jnp.dot(p, v_ref[:n_cols(h), :], preferred_element_type=F32)

        s_next = scores(0)
        pending = None
        done = []
        for h in range(n_chunks + 1):
            if h < n_chunks:
                s_cur = s_next
                if h + 1 < n_chunks:
                    s_next = scores(h + 1)
                m_new, alpha, p = softmax(h, s_cur)
            if pending is not None:
                hp, m_p, alpha_p, p_p = pending
                done.append((hp, m_p, weighted(hp, alpha_p, p_p)))
            pending = (h, m_new, alpha, p)
        for hp, m_p, acc in done:
            m_scr[sls[hp], :] = m_p
            acc_scr[sls[hp], :] = acc

    @pl.when(ki < qi)
    def _():
        block(False)

    @pl.when(ki == qi)
    def _():
        block(True)
        acc = acc_scr[...]
        o_ref[...] = (acc / acc[:, ONES_LANE:ONES_LANE + 1]).astype(BF16)


def _attention(q, kt, v, *, batch, seq, ta):
    T, hw = q.shape
    nq = seq // ta
    kern = functools.partial(_attn_kernel, ta=ta)
    qspec = pl.BlockSpec((ta, HEAD_PAD), lambda b, h, qi, ki: (b * nq + qi, h))
    kspec = pl.BlockSpec((ta, HEAD_PAD), lambda b, h, qi, ki: (b * nq + jnp.minimum(ki, qi), h))
    ktspec = pl.BlockSpec((HEAD_PAD, ta), lambda b, h, qi, ki: (h, b * nq + jnp.minimum(ki, qi)))
    return pl.pallas_call(
        kern,
        grid=(batch, MLA_HEADS, nq, nq),
        in_specs=[qspec, ktspec, kspec],
        out_specs=qspec,
        out_shape=jax.ShapeDtypeStruct((T, hw), BF16),
        scratch_shapes=[pltpu.VMEM((ta, HEAD_PAD), F32)] * 2,
        compiler_params=_cparams(("arbitrary",) * 4),
        name="attention",
    )(q, kt, v)


def _merge_kernel(x_ref, mod_ref, ya_ref, yb_ref, yc_ref, wg_ref, wa_ref, wb_ref, wc_ref, wo_ref,
                  xo_ref):
    D = D_MODEL
    x = x_ref[...]
    m = mod_ref[0]
    sh1, sc1, g1 = m[0:1], m[1:2], m[2:3]
    h = (_rms(x) * (1.0 + sc1) + sh1).astype(BF16)
    gates = jax.nn.sigmoid(jnp.dot(h, wg_ref[...], preferred_element_type=F32))
    merged = gates[:, 0:D] * jnp.dot(ya_ref[...], wa_ref[...], preferred_element_type=F32)
    merged += gates[:, D:2 * D] * jnp.dot(yb_ref[...], wb_ref[...], preferred_element_type=F32)
    merged += gates[:, 2 * D:3 * D] * jnp.dot(yc_ref[...], wc_ref[...], preferred_element_type=F32)
    xo_ref[...] = x + g1 * jnp.dot(merged.astype(BF16), wo_ref[...], preferred_element_type=F32)


def _merge(x, mod, ya, yb, yc, wg, wa, wb, wc, wo, *, seq, tb):
    T, D = x.shape
    full = lambda a: pl.BlockSpec(a.shape, lambda i: (0,) * a.ndim)
    row = lambda w: pl.BlockSpec((tb, w), lambda i: (i, 0))
    return pl.pallas_call(
        _merge_kernel,
        grid=(T // tb,),
        in_specs=[row(D), pl.BlockSpec((1, SUBLANES, D), lambda i: ((i * tb) // seq, 0, 0)),
                  row(ya.shape[1]), row(yb.shape[1]), row(yc.shape[1]),
                  full(wg), full(wa), full(wb), full(wc), full(wo)],
        out_specs=row(D),
        out_shape=jax.ShapeDtypeStruct((T, D), F32),
        compiler_params=_cparams(("arbitrary",)),
        name="merge",
    )(x, mod, ya, yb, yc, wg, wa, wb, wc, wo)


def _top16_keys(s, vals_ref, idx_ref):
    half = s.shape[0] // 2
    a, b = s[:half], s[half:]
    rows = lax.broadcasted_iota(jnp.int32, a.shape, 0).astype(F32)
    a_first = a >= b
    act, act_id = jnp.where(a_first, a, b), jnp.where(a_first, rows, rows + half)
    rest, rest_id = jnp.where(a_first, b, a), jnp.where(a_first, rows + half, rows)
    for j in range(PEER_TOPK):
        mx = jnp.max(act, axis=0, keepdims=True)
        pos = jnp.min(jnp.where(act == mx, act_id, float(s.shape[0])), axis=0, keepdims=True)
        hit = act_id == pos
        vals_ref[j:j + 1, :] = mx
        idx_ref[j:j + 1, :] = pos
        act, act_id = jnp.where(hit, rest, act), jnp.where(hit, rest_id, act_id)
        rest = jnp.where(hit, NEG, rest)


def _top16_payload(s, payload, vals_ref, idx_ref):
    rows = lax.broadcasted_iota(jnp.int32, s.shape, 0).astype(F32)
    for j in range(PEER_TOPK):
        mx = jnp.max(s, axis=0, keepdims=True)
        pos = jnp.min(jnp.where(s == mx, rows, float(s.shape[0])), axis=0, keepdims=True)
        hit = rows == pos
        vals_ref[j:j + 1, :] = mx
        picked = jnp.sum(jnp.where(hit, payload, 0.0), axis=0, keepdims=True)
        idx_ref[j:j + 1, :] = picked.astype(jnp.int32)
        s = jnp.where(hit, NEG, s)


def _route_kernel(x_ref, mod_ref, wpq_ref, keys_ref, h2_ref, idx_ref, g_ref,
                  h2_scr, v1_scr, i1_scr, v2_scr, i2_scr, tv_scr):
    hd = pl.program_id(1)

    @pl.when(hd == 0)
    def _():
        m = mod_ref[0]
        sh2, sc2 = m[3:4], m[4:5]
        h2 = _rms(x_ref[...]) * (1.0 + sc2) + sh2
        h2_ref[...] = h2
        h2_scr[...] = h2.astype(BF16)

    q = jnp.dot(h2_scr[...], wpq_ref[...], preferred_element_type=F32).astype(BF16)
    nt = (((1,), (1,)), ((), ()))
    s1 = lax.dot_general(keys_ref[0, 0], q[:, :PEER_HALF], nt, preferred_element_type=F32)
    s2 = lax.dot_general(keys_ref[0, 1], q[:, PEER_HALF:], nt, preferred_element_type=F32)
    _top16_keys(s1, v1_scr, i1_scr)
    _top16_keys(s2, v2_scr, i2_scr)
    v1, v2 = v1_scr[...], v2_scr[...]
    i1, i2 = i1_scr[...] * N_KEYS, i2_scr[...]

    r8 = lax.broadcasted_iota(jnp.int32, (SUBLANES, v1.shape[1]), 0)
    low = r8 < 4

    def two(a0, a1, nb0, nb1, val, idx):
        va = jnp.where(low, val[a0:a0 + 1], val[a1:a1 + 1])
        ia = jnp.where(low, idx[a0:a0 + 1], idx[a1:a1 + 1])
        ok = r8 < jnp.where(low, nb0, nb1 + 4)
        return va, ia, ok

    v2lo, i2lo = v2[0:SUBLANES], i2[0:SUBLANES]
    v2rep = jnp.where(low, v2lo, pltpu.roll(v2lo, 4, axis=0))
    i2rep = jnp.where(low, i2lo, pltpu.roll(i2lo, 4, axis=0))
    cand = [v1[0:1] + v2[0:SUBLANES], v1[0:1] + v2[SUBLANES:], v1[1:2] + v2lo]
    cidx = [i1[0:1] + i2[0:SUBLANES], i1[0:1] + i2[SUBLANES:], i1[1:2] + i2lo]
    for a, nb in ((2, 5), (3, 4)):
        cand.append(jnp.where(r8 < nb, v1[a:a + 1] + v2lo, NEG))
        cidx.append(i1[a:a + 1] + i2lo)
    for a0, a1, nb0, nb1 in ((4, 5, 3, 2), (6, 7, 2, 2)):
        va, ia, ok = two(a0, a1, nb0, nb1, v1, i1)
        cand.append(jnp.where(ok, va + v2rep, NEG))
        cidx.append(ia + i2rep)
    cand.append(v1[SUBLANES:] + v2[0:1])
    cidx.append(i1[SUBLANES:] + i2[0:1])
    cand = jnp.concatenate(cand, axis=0)
    cidx = jnp.concatenate(cidx, axis=0)
    _top16_payload(cand, cidx, tv_scr, idx_ref)
    tv = tv_scr[...]
    e = jnp.exp(tv - tv[0:1])
    g_ref[...] = e / jnp.sum(e, axis=0, keepdims=True)


def _route(x, mod, wpq, keys, *, seq, tb):
    T, D = x.shape
    kd = 2 * PEER_HALF
    return pl.pallas_call(
        _route_kernel,
        grid=(T // tb, PEER_HEADS),
        in_specs=[
            pl.BlockSpec((tb, D), lambda i, h: (i, 0)),
            pl.BlockSpec((1, SUBLANES, D), lambda i, h: ((i * tb) // seq, 0, 0)),
            pl.BlockSpec((D, kd), lambda i, h: (0, h)),
            pl.BlockSpec((1, 2, N_KEYS, PEER_HALF), lambda i, h: (h, 0, 0, 0)),
        ],
        out_specs=[
            pl.BlockSpec((tb, D), lambda i, h: (i, 0)),
            pl.BlockSpec((PEER_TOPK, tb), lambda i, h: (h, i)),
            pl.BlockSpec((PEER_TOPK, tb), lambda i, h: (h, i)),
        ],
        out_shape=[
            jax.ShapeDtypeStruct((T, D), F32),
            jax.ShapeDtypeStruct((N_PAIRS, T), jnp.int32),
            jax.ShapeDtypeStruct((N_PAIRS, T), F32),
        ],
        scratch_shapes=[
            pltpu.VMEM((tb, D), BF16),
            pltpu.VMEM((PEER_TOPK, tb), F32), pltpu.VMEM((PEER_TOPK, tb), F32),
            pltpu.VMEM((PEER_TOPK, tb), F32), pltpu.VMEM((PEER_TOPK, tb), F32),
            pltpu.VMEM((PEER_TOPK, tb), F32),
        ],
        compiler_params=_cparams(("arbitrary", "arbitrary")),
        name="route",
    )(x, mod, wpq, keys)


def _load_table(tab_hbm, tab_vmem, sem):
    @pl.when(pl.program_id(0) == 0)
    def _():
        cp = pltpu.make_async_copy(tab_hbm, tab_vmem, sem)
        cp.start()
        cp.wait()


HALF_TILE = SUBLANES // 2
HALF_PAIRS = N_PAIRS // 2
W_ROWS = HALF_PAIRS * SUBLANES


def _pair_tiles(tab_vmem, rows_ref, t):
    tiles = []
    for c in range(N_PAIRS // ROW_CHUNK):
        rows_c = rows_ref.at[pl.ds(t * N_PAIRS + c * ROW_CHUNK, ROW_CHUNK)]
        tiles += [tab_vmem[pl.ds(pl.multiple_of(rows_c[j], HALF_TILE), HALF_TILE), :] for j in range(ROW_CHUNK)]
    wa = pltpu.bitcast(jnp.concatenate(tiles[:HALF_PAIRS], axis=0), BF16)
    wb = pltpu.bitcast(jnp.concatenate(tiles[HALF_PAIRS:], axis=0), BF16)
    return jnp.concatenate([wa, wb], axis=1)


ROW_CHUNK = 16
TOKENS_PER_TRIP = 32


def _token_loop(tb, token):
    def trip(i, carry):
        for u in range(TOKENS_PER_TRIP):
            token(i * TOKENS_PER_TRIP + u, carry)
        return carry
    lax.fori_loop(0, tb // TOKENS_PER_TRIP, trip, 0)


def _expansion_constants():
    col = np.arange(2 * W_ROWS)
    pair = (col // W_ROWS) * HALF_PAIRS + (col % W_ROWS) // SUBLANES
    g = np.zeros((2 * W_ROWS, N_PAIRS), np.float32)
    g[col, pair] = 1.0
    return g, g.T.copy()


def _peer_u_kernel(rows_ref, h_ref, g_ref, gsum_ref, spread_ref, tab_hbm, cexp_ref,
                   tab_vmem, stage, sem, *, tb):
    _load_table(tab_hbm, tab_vmem, sem)
    sub = lax.broadcasted_iota(jnp.int32, (2 * SUBLANES, W_ROWS), 0)
    lane = lax.broadcasted_iota(jnp.int32, (2 * SUBLANES, W_ROWS), 1)
    diag = (lane & (SUBLANES - 1)) == (sub & (SUBLANES - 1))
    nt = (((1,), (1,)), ((), ()))

    def token(t, carry):
        w = _pair_tiles(tab_vmem, rows_ref, t)
        hrow = h_ref[pl.ds(t, 1), :]
        xt = jnp.concatenate([hrow[:, r * LANES:(r + 1) * LANES] for r in range(SUBLANES)], axis=0).astype(BF16)
        z = jnp.zeros_like(xt)
        x2 = jnp.concatenate([jnp.concatenate([xt, z], axis=1),
                              jnp.concatenate([z, xt], axis=1)], axis=0)
        res = lax.dot_general(x2, w, nt, preferred_element_type=F32)
        m = jnp.where(diag, res, 0.0)
        stage[pl.ds(t, 1), 0:W_ROWS] = jnp.sum(m[:SUBLANES], axis=0, keepdims=True)
        stage[pl.ds(t, 1), W_ROWS:] = jnp.sum(m[SUBLANES:], axis=0, keepdims=True)
        return carry

    _token_loop(tb, token)
    s = stage[...]
    hi = s.astype(BF16)
    lo = (s - hi.astype(F32)).astype(BF16)
    a = (jnp.dot(hi, gsum_ref[...], preferred_element_type=F32)
         + jnp.dot(lo, gsum_ref[...], preferred_element_type=F32))
    coef = (g_ref[...] * jax.nn.gelu(a)).astype(BF16)
    cexp_ref[...] = jnp.dot(coef, spread_ref[...], preferred_element_type=F32)


def _peer_u(rows_tm, h2, g_tm, gsum, spread, tab, *, tb):
    T = h2.shape[0]
    kern = functools.partial(_peer_u_kernel, tb=tb)
    full = lambda a: pl.BlockSpec(a.shape, lambda i: (0,) * a.ndim)
    return pl.pallas_call(
        kern,
        grid=(T // tb,),
        in_specs=[
            pl.BlockSpec((tb * N_PAIRS,), lambda i: (i,), memory_space=pltpu.SMEM),
            pl.BlockSpec((tb, D_MODEL), lambda i: (i, 0)),
            pl.BlockSpec((tb, N_PAIRS), lambda i: (i, 0)),
            full(gsum), full(spread),
            pl.BlockSpec(memory_space=pl.ANY),
        ],
        out_specs=pl.BlockSpec((tb, 2 * W_ROWS), lambda i: (i, 0)),
        out_shape=jax.ShapeDtypeStruct((T, 2 * W_ROWS), F32),
        scratch_shapes=[
            pltpu.VMEM((N_EXPERTS * HALF_TILE, LANES), jnp.uint32),
            pltpu.VMEM((tb, 2 * W_ROWS), F32),
            pltpu.SemaphoreType.DMA,
        ],
        compiler_params=_cparams(("arbitrary",), TABLE_VMEM_LIMIT),
        name="peer_u",
    )(rows_tm, h2, g_tm, gsum, spread, tab)


def _peer_v_kernel(rows_ref, cexp_ref, x_ref, g2_ref, tab_hbm, xo_ref, tab_vmem, sem, *, tb):
    _load_table(tab_hbm, tab_vmem, sem)
    g2 = g2_ref[0]
    sub = lax.broadcasted_iota(jnp.int32, (SUBLANES, LANES), 0)
    lane = lax.broadcasted_iota(jnp.int32, (SUBLANES, LANES), 1)
    diag = (lane & (SUBLANES - 1)) == sub
    n_chunks = W_ROWS // LANES

    def token(t, carry):
        w = _pair_tiles(tab_vmem, rows_ref, t)
        crow = cexp_ref[pl.ds(t, 1), :]
        blocks = []
        for blk in range(2):
            lo = [(blk * n_chunks + q) * LANES for q in range(n_chunks)]
            chunks = [jnp.where(diag, jnp.broadcast_to(crow[:, c:c + LANES], diag.shape), 0.0) for c in lo]
            blocks.append(jnp.concatenate(chunks, axis=1))
        c = jnp.concatenate(blocks, axis=0).astype(BF16)
        res = jnp.dot(c, w, preferred_element_type=F32)
        out = g2 * (res[:SUBLANES, :LANES] + res[SUBLANES:, LANES:])
        delta = jnp.concatenate([out[r:r + 1, :] for r in range(SUBLANES)], axis=1)
        xo_ref[pl.ds(t, 1), :] = x_ref[pl.ds(t, 1), :] + delta
        return carry

    _token_loop(tb, token)


def _peer_v(rows_tm, cexp, x, g2, tab, *, seq, tb):
    T = x.shape[0]
    kern = functools.partial(_peer_v_kernel, tb=tb)
    tok = pl.BlockSpec((tb, D_MODEL), lambda i: (i, 0))
    return pl.pallas_call(
        kern,
        grid=(T // tb,),
        in_specs=[
            pl.BlockSpec((tb * N_PAIRS,), lambda i: (i,), memory_space=pltpu.SMEM),
            pl.BlockSpec((tb, 2 * W_ROWS), lambda i: (i, 0)),
            tok,
            pl.BlockSpec((1, SUBLANES, LANES), lambda i: ((i * tb) // seq, 0, 0)),
            pl.BlockSpec(memory_space=pl.ANY),
        ],
        out_specs=tok,
        out_shape=jax.ShapeDtypeStruct(x.shape, F32),
        scratch_shapes=[
            pltpu.VMEM((N_EXPERTS * HALF_TILE, LANES), jnp.uint32),
            pltpu.SemaphoreType.DMA,
        ],
        compiler_params=_cparams(("arbitrary",), TABLE_VMEM_LIMIT),
        name="peer_v",
    )(rows_tm, cexp, x, g2, tab)


def _final_kernel(x_ref, g_ref, o_ref):
    o_ref[...] = _rms(x_ref[...]) * g_ref[...]


def _final_norm(x, gain, tb):
    T, D = x.shape
    return pl.pallas_call(
        _final_kernel,
        grid=(T // tb,),
        in_specs=[pl.BlockSpec((tb, D), lambda i: (i, 0)), pl.BlockSpec((1, D), lambda i: (0, 0))],
        out_specs=pl.BlockSpec((tb, D), lambda i: (i, 0)),
        out_shape=jax.ShapeDtypeStruct((T, D), F32),
        compiler_params=_cparams(("arbitrary",)),
        name="final_norm",
    )(x, gain)


PACK_ROWS = 512


def _pack_kernel(x_ref, o_ref):
    bits = pltpu.bitcast(x_ref[...].astype(BF16).astype(F32), jnp.uint32)
    for s in range(HALF_TILE):
        lo = bits[:, (2 * s) * LANES:(2 * s + 1) * LANES] >> 16
        hi = bits[:, (2 * s + 1) * LANES:(2 * s + 2) * LANES] & jnp.uint32(0xFFFF0000)
        o_ref[pl.ds(s, PACK_ROWS, stride=HALF_TILE), :] = hi | lo


def _pack_table(tab):
    lead = tab.shape[:-2]
    flat = tab.reshape(-1, D_MODEL)
    n = flat.shape[0]
    packed = pl.pallas_call(
        _pack_kernel,
        grid=(n // PACK_ROWS,),
        in_specs=[pl.BlockSpec((PACK_ROWS, D_MODEL), lambda i: (i, 0))],
        out_specs=pl.BlockSpec((PACK_ROWS * HALF_TILE, LANES), lambda i: (i, 0)),
        out_shape=jax.ShapeDtypeStruct((n * HALF_TILE, LANES), jnp.uint32),
        compiler_params=_cparams(("arbitrary",)),
        name="pack_table",
    )(flat)
    return packed.reshape(*lead, N_EXPERTS * HALF_TILE, LANES)


def _pad_heads(w, per_head, offset):
    k = w.shape[0]
    w = w.reshape(k, MLA_HEADS, per_head)
    out = jnp.zeros((k, MLA_HEADS, HEAD_PAD), w.dtype)
    out = out.at[:, :, offset:offset + per_head].set(w)
    return out.reshape(k, MLA_HEADS * HEAD_PAD)


def _block_size(seq, want):
    tb = min(want, seq)
    assert seq % tb == 0
    return tb


def _layer(xf, mod_l, cos, sin, w_in, w_s, b_s, q_norm, w_uq, kv_norm, w_ukv, w_pool, pool_scale,
           w_a, w_b, w_c, w_o, w_pq, sub_keys, u_packed, v_packed, *, batch, seq):
    T, D = xf.shape
    tb_in = _block_size(seq, 1024)
    tb_mg = _block_size(seq, 512)
    tb_rt = _block_size(seq, 1024)
    ta = _block_size(seq, 2048)
    tb_peer = LANES

    sp = (0, 256, 512, 896, 1152, 1184, 1440)
    kr_pad = jnp.zeros((D, LANES), F32).at[:, QK_NOPE:QK_NOPE + QK_ROPE].set(w_in[:, sp[4]:sp[5]])
    w1 = jnp.concatenate([w_in[:, :sp[4]], w_in[:, sp[5]:sp[6]], kr_pad], axis=1).astype(BF16)
    wg = w_in[:, sp[6]:].astype(BF16)
    bs_full = jnp.repeat(b_s.T, A_GROUP_DIM, axis=1)
    wuq = _pad_heads(w_uq, QK_DIM, 0).astype(BF16)
    wukv = w_ukv.reshape(KV_LORA, MLA_HEADS, QK_NOPE + V_DIM)
    wk = _pad_heads(wukv[:, :, :QK_NOPE].reshape(KV_LORA, -1), QK_NOPE, 0).astype(BF16)
    wv = _pad_heads(wukv[:, :, QK_NOPE:].reshape(KV_LORA, -1), V_DIM, 0).astype(BF16)
    wpool = jax.scipy.linalg.block_diag(*[w_pool[g] for g in range(len(POOL_WINDOWS))]).astype(BF16)
    wb = w_b.reshape(MLA_HEADS, V_DIM, D)
    wb = jnp.pad(wb, ((0, 0), (0, HEAD_PAD - V_DIM), (0, 0))).reshape(MLA_HEADS * HEAD_PAD, D).astype(BF16)

    ya, yc, q, k, v = _inproj(
        xf, mod_l, w1, w_s, bs_full, q_norm.reshape(1, -1), wuq, kv_norm.reshape(1, -1),
        wk, wv, cos, sin, wpool, pool_scale.reshape(1, -1), seq=seq, tb=tb_in)
    yb = _attention(q, k.T, v, batch=batch, seq=seq, ta=ta)
    x1 = _merge(xf, mod_l, ya, yb, yc, wg, w_a.astype(BF16), wb, w_c.astype(BF16),
                w_o.astype(BF16), seq=seq, tb=tb_mg)
    h2, idx_t, g_t = _route(x1, mod_l, w_pq.astype(BF16), sub_keys.astype(BF16), seq=seq, tb=tb_rt)
    rows_tm = (idx_t.T * HALF_TILE).reshape(-1)
    gsum, spread = (jnp.asarray(a, BF16) for a in _expansion_constants())
    cexp = _peer_u(rows_tm, h2, g_t.T, gsum, spread,
                   u_packed, tb=tb_peer)
    g2 = mod_l[:, 5].reshape(batch, SUBLANES, LANES)
    x2 = _peer_v(rows_tm, cexp, x1, g2, v_packed, seq=seq, tb=tb_peer)
    return x2, (x1, h2)


def kernel(x, c, positions, w_mod, b_mod, w_in, w_s, b_s, q_norm, w_uq, kv_norm, w_ukv, w_pool, pool_scale, w_a, w_b, w_c, w_o, w_pq, sub_keys, u_tab, v_tab, final_norm):
    B, S, D = x.shape
    L = w_mod.shape[0]
    T = B * S
    assert D == D_MODEL and S % A_CHUNK == 0

    c_pad = jnp.zeros((SUBLANES, D), F32).at[:B].set(c)
    mod = _modulation(c_pad, w_mod, b_mod)[:, :B].reshape(L, B, 6, D)
    mod = jnp.pad(mod, ((0, 0), (0, 0), (0, SUBLANES - 6), (0, 0)))

    inv_freq = 1.0 / (ROPE_THETA ** (jnp.arange(0, QK_ROPE, 2, dtype=F32) / QK_ROPE))
    half = QK_ROPE // 2
    freq_lane = jnp.zeros((LANES,), F32)
    freq_lane = freq_lane.at[QK_NOPE:QK_NOPE + half].set(inv_freq)
    freq_lane = freq_lane.at[QK_NOPE + half:QK_NOPE + QK_ROPE].set(inv_freq)
    ang = positions.astype(F32).reshape(T, 1) * freq_lane[None, :]
    cos, sin = _rope_tables(ang, _block_size(T, 1024))

    u_packed = _pack_table(u_tab)
    v_packed = _pack_table(v_tab)
    xf = x.reshape(T, D)
    for l in range(L):
        xf, _ = _layer(xf, mod[l], cos, sin, w_in[l], w_s[l], b_s[l], q_norm[l], w_uq[l], kv_norm[l],
                       w_ukv[l], w_pool[l], pool_scale[l], w_a[l], w_b[l], w_c[l], w_o[l], w_pq[l],
                       sub_keys[l], u_packed[l], v_packed[l], batch=B, seq=S)

    out = _final_norm(xf, final_norm.reshape(1, D), _block_size(T, 1024))
    return out.reshape(B, S, D)
```

```python
import functools
import math

import numpy as np
import jax
import jax.numpy as jnp
from jax import lax
from jax.experimental import pallas as pl
from jax.experimental.pallas import tpu as pltpu

F32 = jnp.float32
BF16 = jnp.bfloat16

EPS = 1e-6
D_MODEL = 1024
A_WIDTH = 256
A_GROUPS = 4
A_GROUP_DIM = A_WIDTH // A_GROUPS
A_CHUNK = 128
MLA_HEADS = 8
Q_LORA = 384
KV_LORA = 256
QK_NOPE = 64
QK_ROPE = 32
V_DIM = 64
QK_DIM = QK_NOPE + QK_ROPE
ROPE_THETA = 10000.0
POOL_WINDOWS = (2, 4, 8, 16)
C_WIDTH = 256
C_GROUP = C_WIDTH // len(POOL_WINDOWS)
PEER_HEADS = 8
N_KEYS = 128
N_EXPERTS = N_KEYS * N_KEYS
PEER_HALF = 128
PEER_TOPK = 16
N_PAIRS = PEER_HEADS * PEER_TOPK

LANES = 128
SUBLANES = 8
HEAD_PAD = LANES
HALO = 16
NEG = -1e30
ATT_CHUNKS = 8
ONES_LANE = V_DIM

VMEM_LIMIT = 48 * 1024 * 1024
TABLE_VMEM_LIMIT = 56 * 1024 * 1024

_C_ZU, _C_ZV, _C_CQ, _C_CKV, _C_ZP, _C_KR, _C_END = 0, 256, 512, 896, 1152, 1408, 1536


def _rms(x):
    return x * lax.rsqrt(jnp.mean(x * x, axis=-1, keepdims=True) + EPS)


def _cparams(sem, limit=VMEM_LIMIT):
    return pltpu.CompilerParams(dimension_semantics=sem, vmem_limit_bytes=limit)


def _mod_kernel(c_ref, w_ref, b_ref, o_ref):
    c = c_ref[...]
    ca = c * jax.nn.sigmoid(c)
    o_ref[0] = jnp.dot(ca, w_ref[0], preferred_element_type=F32) + b_ref[0]


def _modulation(c_pad, w_mod, b_mod):
    L, D, six_d = w_mod.shape
    nb = six_d // D
    rows = c_pad.shape[0]
    return pl.pallas_call(
        _mod_kernel,
        grid=(L, nb),
        in_specs=[
            pl.BlockSpec((rows, D), lambda l, j: (0, 0)),
            pl.BlockSpec((1, D, D), lambda l, j: (l, 0, j)),
            pl.BlockSpec((1, 1, D), lambda l, j: (l, 0, j)),
        ],
        out_specs=pl.BlockSpec((1, rows, D), lambda l, j: (l, 0, j)),
        out_shape=jax.ShapeDtypeStruct((L, rows, six_d), F32),
        compiler_params=_cparams(("arbitrary", "arbitrary")),
        name="modulation",
    )(c_pad, w_mod, b_mod.reshape(L, 1, six_d))


def _rope_kernel(ang_ref, cos_ref, sin_ref):
    ang = ang_ref[...]
    lane = lax.broadcasted_iota(jnp.int32, ang.shape, 1)
    first_half = lane < QK_NOPE + QK_ROPE // 2
    cos_ref[...] = jnp.cos(ang)
    s = jnp.sin(ang)
    sin_ref[...] = jnp.where(first_half, -s, s)


def _rope_tables(ang, tb):
    T = ang.shape[0]
    spec = pl.BlockSpec((tb, LANES), lambda i: (i, 0))
    return pl.pallas_call(
        _rope_kernel,
        grid=(T // tb,),
        in_specs=[spec],
        out_specs=[spec, spec],
        out_shape=[jax.ShapeDtypeStruct((T, LANES), F32)] * 2,
        compiler_params=_cparams(("arbitrary",)),
        name="rope_tables",
    )(ang)


def _rope_apply(xh, cos, sin):
    lane = lax.broadcasted_iota(jnp.int32, xh.shape, 1)
    half = QK_ROPE // 2
    partner = jnp.where(lane < QK_NOPE + half,
                        pltpu.roll(xh, LANES - half, axis=1),
                        pltpu.roll(xh, half, axis=1))
    return xh * cos + partner * sin


def _inproj_kernel(x_ref, xh_ref, mod_ref, w1_ref, ws_ref, bs_ref, qn_ref, wuq_ref, kvn_ref,
                   wk_ref, wv_ref, cos_ref, sin_ref, wpool_ref, pscale_ref,
                   ya_ref, yc_ref, q_ref, k_ref, v_ref, *, seq, tb):
    i = pl.program_id(0)
    m = mod_ref[0]
    sh1, sc1 = m[0:1], m[1:2]
    h = (_rms(x_ref[...]) * (1.0 + sc1) + sh1).astype(BF16)
    z = jnp.dot(h, w1_ref[...], preferred_element_type=F32)

    gu = jax.nn.gelu(z[:, _C_ZU:_C_ZV])
    gv = jax.nn.gelu(z[:, _C_ZV:_C_CQ])
    mu = jnp.mean(gv, axis=-1, keepdims=True)
    dv = gv - mu
    vn = (dv * lax.rsqrt(jnp.mean(dv * dv, axis=-1, keepdims=True) + EPS)).astype(BF16)
    r = lax.broadcasted_iota(jnp.int32, (A_CHUNK, A_CHUNK), 0)
    cc = lax.broadcasted_iota(jnp.int32, (A_CHUNK, A_CHUNK), 1)
    lane_a = lax.broadcasted_iota(jnp.int32, (A_CHUNK, A_WIDTH), 1)
    wtril = [jnp.where(r >= cc, ws_ref[g], 0.0).astype(BF16) for g in range(A_GROUPS)]
    for ci in range(tb // A_CHUNK):
        lo = ci * A_CHUNK
        vc = vn[lo:lo + A_CHUNK]
        mixed = bs_ref[...]
        for g in range(A_GROUPS):
            dg = jnp.dot(wtril[g], vc, preferred_element_type=F32)
            in_g = (lane_a >= g * A_GROUP_DIM) & (lane_a < (g + 1) * A_GROUP_DIM)
            mixed = mixed + jnp.where(in_g, dg, 0.0)
        ya_ref[lo:lo + A_CHUNK, :] = (gu[lo:lo + A_CHUNK] * mixed).astype(BF16)

    zp = z[:, _C_ZP:_C_KR]
    t0 = (i * tb) % seq
    hh = (_rms(xh_ref[...]) * (1.0 + sc1) + sh1).astype(BF16)
    zph = jnp.dot(hh, w1_ref[:, _C_ZP:_C_KR], preferred_element_type=F32)
    zph = zph * jnp.where(t0 == 0, 0.0, 1.0)
    zext = jnp.concatenate([zph, zp], axis=0)
    n_ext = HALO + tb
    s2 = zext + pltpu.roll(zext, 1, axis=0)
    s4 = s2 + pltpu.roll(s2, 2, axis=0)
    s8 = s4 + pltpu.roll(s4, 4, axis=0)
    s16 = s8 + pltpu.roll(s8, 8, axis=0)
    lane_c = lax.broadcasted_iota(jnp.int32, (n_ext, C_WIDTH), 1)
    grp = lane_c // C_GROUP
    win = jnp.where(grp == 0, s2, jnp.where(grp == 1, s4, jnp.where(grp == 2, s8, s16)))
    wsize = jnp.where(grp == 0, 2, jnp.where(grp == 1, 4, jnp.where(grp == 2, 8, 16)))
    tpos = t0 - HALO + lax.broadcasted_iota(jnp.int32, (n_ext, C_WIDTH), 0)
    cnt = jnp.minimum(tpos + 1, wsize).astype(F32)
    pooled = (win / jnp.maximum(cnt, 1.0) - zext)[HALO:]
    yc = jnp.dot(pooled.astype(BF16), wpool_ref[...], preferred_element_type=F32)
    yc_ref[...] = (yc * pscale_ref[...]).astype(BF16)

    cos = cos_ref[...]
    sin = sin_ref[...]
    cqn = (_rms(z[:, _C_CQ:_C_CKV]) * qn_ref[...]).astype(BF16)
    q = jnp.dot(cqn, wuq_ref[...], preferred_element_type=F32)
    ckvn = (_rms(z[:, _C_CKV:_C_ZP]) * kvn_ref[...]).astype(BF16)
    kn = jnp.dot(ckvn, wk_ref[...], preferred_element_type=F32)
    vv = jnp.dot(ckvn, wv_ref[...], preferred_element_type=F32)
    lane_v = lax.broadcasted_iota(jnp.int32, vv.shape, 1)
    v_ref[...] = jnp.where(lane_v % HEAD_PAD == ONES_LANE, 1.0, vv).astype(BF16)
    krope = _rope_apply(z[:, _C_KR:_C_END], cos, sin)
    qscale = math.log2(math.e) / math.sqrt(QK_DIM)
    for hd in range(MLA_HEADS):
        sl = slice(hd * HEAD_PAD, (hd + 1) * HEAD_PAD)
        q_ref[:, sl] = (_rope_apply(q[:, sl], cos, sin) * qscale).astype(BF16)
        k_ref[:, sl] = (kn[:, sl] + krope).astype(BF16)


def _inproj(x, mod, w1, w_s, bs_full, q_norm, wuq, kv_norm, wk, wv, cos, sin, wpool, pscale, *, seq, tb):
    T, D = x.shape
    hpb = tb // HALO
    kern = functools.partial(_inproj_kernel, seq=seq, tb=tb)
    full = lambda a: pl.BlockSpec(a.shape, lambda i: (0,) * a.ndim)
    hw = MLA_HEADS * HEAD_PAD
    return pl.pallas_call(
        kern,
        grid=(T // tb,),
        in_specs=[
            pl.BlockSpec((tb, D), lambda i: (i, 0)),
            pl.BlockSpec((HALO, D), lambda i: (jnp.maximum(i * hpb - 1, 0), 0)),
            pl.BlockSpec((1, SUBLANES, D), lambda i: ((i * tb) // seq, 0, 0)),
            full(w1), full(w_s), full(bs_full), full(q_norm), full(wuq), full(kv_norm),
            full(wk), full(wv),
            pl.BlockSpec((tb, LANES), lambda i: (i, 0)),
            pl.BlockSpec((tb, LANES), lambda i: (i, 0)),
            full(wpool), full(pscale),
        ],
        out_specs=[
            pl.BlockSpec((tb, A_WIDTH), lambda i: (i, 0)),
            pl.BlockSpec((tb, C_WIDTH), lambda i: (i, 0)),
            pl.BlockSpec((tb, hw), lambda i: (i, 0)),
            pl.BlockSpec((tb, hw), lambda i: (i, 0)),
            pl.BlockSpec((tb, hw), lambda i: (i, 0)),
        ],
        out_shape=[
            jax.ShapeDtypeStruct((T, A_WIDTH), BF16),
            jax.ShapeDtypeStruct((T, C_WIDTH), BF16),
            jax.ShapeDtypeStruct((T, hw), BF16),
            jax.ShapeDtypeStruct((T, hw), BF16),
            jax.ShapeDtypeStruct((T, hw), BF16),
        ],
        compiler_params=_cparams(("arbitrary",)),
        name="inproj",
    )(x, x, mod, w1, w_s, bs_full, q_norm, wuq, kv_norm, wk, wv, cos, sin, wpool, pscale)


def _attn_kernel(q_ref, kt_ref, v_ref, o_ref, m_scr, acc_scr, *, ta):
    qi = pl.program_id(2)
    ki = pl.program_id(3)

    @pl.when(ki == 0)
    def _():
        m_scr[...] = jnp.full(m_scr.shape, NEG, F32)
        acc_scr[...] = jnp.zeros(acc_scr.shape, F32)

    def block(diagonal):
        n_chunks = ATT_CHUNKS if ta % (ATT_CHUNKS * SUBLANES) == 0 else 1
        rows = ta // n_chunks
        sls = [slice(h * rows, (h + 1) * rows) for h in range(n_chunks)]
        m_prevs = [m_scr[sl, :] for sl in sls]
        acc_prevs = [acc_scr[sl, :] for sl in sls]

        def n_cols(h):
            return (h + 1) * rows if diagonal else ta

        def scores(h):
            s = jnp.dot(q_ref[sls[h], :], kt_ref[:, :n_cols(h)], preferred_element_type=F32)
            if diagonal:
                row = h * rows + lax.broadcasted_iota(jnp.int32, s.shape, 0)
                col = lax.broadcasted_iota(jnp.int32, s.shape, 1)
                s = jnp.where(col <= row, s, NEG)
            return s

        def softmax(h, s):
            m_new = jnp.maximum(m_prevs[h], jnp.max(s, axis=1, keepdims=True))
            alpha = jnp.exp2(m_prevs[h] - m_new)
            return m_new, alpha, jnp.exp2(s - m_new[:, 0:1]).astype(BF16)

        def weighted(h, alpha, p):
            return alpha * acc_prevs[h] + jnp.dot(p, v_ref[:n_cols(h), :], preferred_element_type=F32)

        s_next = scores(0)
        pending = None
        done = []
        for h in range(n_chunks + 1):
            if h < n_chunks:
                s_cur = s_next
                if h + 1 < n_chunks:
                    s_next = scores(h + 1)
                m_new, alpha, p = softmax(h, s_cur)
            if pending is not None:
                hp, m_p, alpha_p, p_p = pending
                done.append((hp, m_p, weighted(hp, alpha_p, p_p)))
            pending = (h, m_new, alpha, p)
        for hp, m_p, acc in done:
            m_scr[sls[hp], :] = m_p
            acc_scr[sls[hp], :] = acc

    @pl.when(ki < qi)
    def _():
        block(False)

    @pl.when(ki == qi)
    def _():
        block(True)
        acc = acc_scr[...]
        o_ref[...] = (acc / acc[:, ONES_LANE:ONES_LANE + 1]).astype(BF16)


def _attention(q, kt, v, *, batch, seq, ta):
    T, hw = q.shape
    nq = seq // ta
    kern = functools.partial(_attn_kernel, ta=ta)
    qspec = pl.BlockSpec((ta, HEAD_PAD), lambda b, h, qi, ki: (b * nq + qi, h))
    kspec = pl.BlockSpec((ta, HEAD_PAD), lambda b, h, qi, ki: (b * nq + jnp.minimum(ki, qi), h))
    ktspec = pl.BlockSpec((HEAD_PAD, ta), lambda b, h, qi, ki: (h, b * nq + jnp.minimum(ki, qi)))
    return pl.pallas_call(
        kern,
        grid=(batch, MLA_HEADS, nq, nq),
        in_specs=[qspec, ktspec, kspec],
        out_specs=qspec,
        out_shape=jax.ShapeDtypeStruct((T, hw), BF16),
        scratch_shapes=[pltpu.VMEM((ta, HEAD_PAD), F32)] * 2,
        compiler_params=_cparams(("arbitrary",) * 4),
        name="attention",
    )(q, kt, v)


def _merge_kernel(x_ref, mod_ref, ya_ref, yb_ref, yc_ref, wg_ref, wa_ref, wb_ref, wc_ref, wo_ref,
                  xo_ref):
    D = D_MODEL
    x = x_ref[...]
    m = mod_ref[0]
    sh1, sc1, g1 = m[0:1], m[1:2], m[2:3]
    h = (_rms(x) * (1.0 + sc1) + sh1).astype(BF16)
    gates = jax.nn.sigmoid(jnp.dot(h, wg_ref[...], preferred_element_type=F32))
    merged = gates[:, 0:D] * jnp.dot(ya_ref[...], wa_ref[...], preferred_element_type=F32)
    merged += gates[:, D:2 * D] * jnp.dot(yb_ref[...], wb_ref[...], preferred_element_type=F32)
    merged += gates[:, 2 * D:3 * D] * jnp.dot(yc_ref[...], wc_ref[...], preferred_element_type=F32)
    xo_ref[...] = x + g1 * jnp.dot(merged.astype(BF16), wo_ref[...], preferred_element_type=F32)


def _merge(x, mod, ya, yb, yc, wg, wa, wb, wc, wo, *, seq, tb):
    T, D = x.shape
    full = lambda a: pl.BlockSpec(a.shape, lambda i: (0,) * a.ndim)
    row = lambda w: pl.BlockSpec((tb, w), lambda i: (i, 0))
    return pl.pallas_call(
        _merge_kernel,
        grid=(T // tb,),
        in_specs=[row(D), pl.BlockSpec((1, SUBLANES, D), lambda i: ((i * tb) // seq, 0, 0)),
                  row(ya.shape[1]), row(yb.shape[1]), row(yc.shape[1]),
                  full(wg), full(wa), full(wb), full(wc), full(wo)],
        out_specs=row(D),
        out_shape=jax.ShapeDtypeStruct((T, D), F32),
        compiler_params=_cparams(("arbitrary",)),
        name="merge",
    )(x, mod, ya, yb, yc, wg, wa, wb, wc, wo)


def _top16_keys(s, vals_ref, idx_ref):
    half = s.shape[0] // 2
    a, b = s[:half], s[half:]
    rows = lax.broadcasted_iota(jnp.int32, a.shape, 0).astype(F32)
    a_first = a >= b
    act, act_id = jnp.where(a_first, a, b), jnp.where(a_first, rows, rows + half)
    rest, rest_id = jnp.where(a_first, b, a), jnp.where(a_first, rows + half, rows)
    for j in range(PEER_TOPK):
        mx = jnp.max(act, axis=0, keepdims=True)
        pos = jnp.min(jnp.where(act == mx, act_id, float(s.shape[0])), axis=0, keepdims=True)
        hit = act_id == pos
        vals_ref[j:j + 1, :] = mx
        idx_ref[j:j + 1, :] = pos
        act, act_id = jnp.where(hit, rest, act), jnp.where(hit, rest_id, act_id)
        rest = jnp.where(hit, NEG, rest)


def _top16_payload(s, payload, vals_ref, idx_ref):
    rows = lax.broadcasted_iota(jnp.int32, s.shape, 0).astype(F32)
    for j in range(PEER_TOPK):
        mx = jnp.max(s, axis=0, keepdims=True)
        pos = jnp.min(jnp.where(s == mx, rows, float(s.shape[0])), axis=0, keepdims=True)
        hit = rows == pos
        vals_ref[j:j + 1, :] = mx
        picked = jnp.sum(jnp.where(hit, payload, 0.0), axis=0, keepdims=True)
        idx_ref[j:j + 1, :] = picked.astype(jnp.int32)
        s = jnp.where(hit, NEG, s)


def _route_kernel(x_ref, mod_ref, wpq_ref, keys_ref, h2_ref, idx_ref, g_ref,
                  h2_scr, v1_scr, i1_scr, v2_scr, i2_scr, tv_scr):
    hd = pl.program_id(1)

    @pl.when(hd == 0)
    def _():
        m = mod_ref[0]
        sh2, sc2 = m[3:4], m[4:5]
        h2 = _rms(x_ref[...]) * (1.0 + sc2) + sh2
        h2_ref[...] = h2
        h2_scr[...] = h2.astype(BF16)

    q = jnp.dot(h2_scr[...], wpq_ref[...], preferred_element_type=F32).astype(BF16)
    nt = (((1,), (1,)), ((), ()))
    s1 = lax.dot_general(keys_ref[0, 0], q[:, :PEER_HALF], nt, preferred_element_type=F32)
    s2 = lax.dot_general(keys_ref[0, 1], q[:, PEER_HALF:], nt, preferred_element_type=F32)
    _top16_keys(s1, v1_scr, i1_scr)
    _top16_keys(s2, v2_scr, i2_scr)
    v1, v2 = v1_scr[...], v2_scr[...]
    i1, i2 = i1_scr[...] * N_KEYS, i2_scr[...]

    r8 = lax.broadcasted_iota(jnp.int32, (SUBLANES, v1.shape[1]), 0)
    low = r8 < 4

    def two(a0, a1, nb0, nb1, val, idx):
        va = jnp.where(low, val[a0:a0 + 1], val[a1:a1 + 1])
        ia = jnp.where(low, idx[a0:a0 + 1], idx[a1:a1 + 1])
        ok = r8 < jnp.where(low, nb0, nb1 + 4)
        return va, ia, ok

    v2lo, i2lo = v2[0:SUBLANES], i2[0:SUBLANES]
    v2rep = jnp.where(low, v2lo, pltpu.roll(v2lo, 4, axis=0))
    i2rep = jnp.where(low, i2lo, pltpu.roll(i2lo, 4, axis=0))
    cand = [v1[0:1] + v2[0:SUBLANES], v1[0:1] + v2[SUBLANES:], v1[1:2] + v2lo]
    cidx = [i1[0:1] + i2[0:SUBLANES], i1[0:1] + i2[SUBLANES:], i1[1:2] + i2lo]
    for a, nb in ((2, 5), (3, 4)):
        cand.append(jnp.where(r8 < nb, v1[a:a + 1] + v2lo, NEG))
        cidx.append(i1[a:a + 1] + i2lo)
    for a0, a1, nb0, nb1 in ((4, 5, 3, 2), (6, 7, 2, 2)):
        va, ia, ok = two(a0, a1, nb0, nb1, v1, i1)
        cand.append(jnp.where(ok, va + v2rep, NEG))
        cidx.append(ia + i2rep)
    cand.append(v1[SUBLANES:] + v2[0:1])
    cidx.append(i1[SUBLANES:] + i2[0:1])
    cand = jnp.concatenate(cand, axis=0)
    cidx = jnp.concatenate(cidx, axis=0)
    _top16_payload(cand, cidx, tv_scr, idx_ref)
    tv = tv_scr[...]
    e = jnp.exp(tv - tv[0:1])
    g_ref[...] = e / jnp.sum(e, axis=0, keepdims=True)


def _route(x, mod, wpq, keys, *, seq, tb):
    T, D = x.shape
    kd = 2 * PEER_HALF
    return pl.pallas_call(
        _route_kernel,
        grid=(T // tb, PEER_HEADS),
        in_specs=[
            pl.BlockSpec((tb, D), lambda i, h: (i, 0)),
            pl.BlockSpec((1, SUBLANES, D), lambda i, h: ((i * tb) // seq, 0, 0)),
            pl.BlockSpec((D, kd), lambda i, h: (0, h)),
            pl.BlockSpec((1, 2, N_KEYS, PEER_HALF), lambda i, h: (h, 0, 0, 0)),
        ],
        out_specs=[
            pl.BlockSpec((tb, D), lambda i, h: (i, 0)),
            pl.BlockSpec((PEER_TOPK, tb), lambda i, h: (h, i)),
            pl.BlockSpec((PEER_TOPK, tb), lambda i, h: (h, i)),
        ],
        out_shape=[
            jax.ShapeDtypeStruct((T, D), F32),
            jax.ShapeDtypeStruct((N_PAIRS, T), jnp.int32),
            jax.ShapeDtypeStruct((N_PAIRS, T), F32),
        ],
        scratch_shapes=[
            pltpu.VMEM((tb, D), BF16),
            pltpu.VMEM((PEER_TOPK, tb), F32), pltpu.VMEM((PEER_TOPK, tb), F32),
            pltpu.VMEM((PEER_TOPK, tb), F32), pltpu.VMEM((PEER_TOPK, tb), F32),
            pltpu.VMEM((PEER_TOPK, tb), F32),
        ],
        compiler_params=_cparams(("arbitrary", "arbitrary")),
        name="route",
    )(x, mod, wpq, keys)


def _load_table(tab_hbm, tab_vmem, sem):
    @pl.when(pl.program_id(0) == 0)
    def _():
        cp = pltpu.make_async_copy(tab_hbm, tab_vmem, sem)
        cp.start()
        cp.wait()


HALF_TILE = SUBLANES // 2
HALF_PAIRS = N_PAIRS // 2
W_ROWS = HALF_PAIRS * SUBLANES


def _pair_tiles(tab_vmem, rows_ref, t):
    tiles = []
    for c in range(N_PAIRS // ROW_CHUNK):
        rows_c = rows_ref.at[pl.ds(t * N_PAIRS + c * ROW_CHUNK, ROW_CHUNK)]
        tiles += [tab_vmem[pl.ds(pl.multiple_of(rows_c[j], HALF_TILE), HALF_TILE), :] for j in range(ROW_CHUNK)]
    wa = pltpu.bitcast(jnp.concatenate(tiles[:HALF_PAIRS], axis=0), BF16)
    wb = pltpu.bitcast(jnp.concatenate(tiles[HALF_PAIRS:], axis=0), BF16)
    return jnp.concatenate([wa, wb], axis=1)


ROW_CHUNK = 16
TOKENS_PER_TRIP = 64


def _token_loop(tb, token):
    def trip(i, carry):
        for u in range(TOKENS_PER_TRIP):
            token(i * TOKENS_PER_TRIP + u, carry)
        return carry
    lax.fori_loop(0, tb // TOKENS_PER_TRIP, trip, 0)


def _expansion_constants():
    col = np.arange(2 * W_ROWS)
    pair = (col // W_ROWS) * HALF_PAIRS + (col % W_ROWS) // SUBLANES
    g = np.zeros((2 * W_ROWS, N_PAIRS), np.float32)
    g[col, pair] = 1.0
    return g, g.T.copy()


def _peer_u_kernel(rows_ref, h_ref, g_ref, gsum_ref, spread_ref, tab_hbm, cexp_ref,
                   tab_vmem, stage, sem, *, tb):
    _load_table(tab_hbm, tab_vmem, sem)
    sub = lax.broadcasted_iota(jnp.int32, (2 * SUBLANES, W_ROWS), 0)
    lane = lax.broadcasted_iota(jnp.int32, (2 * SUBLANES, W_ROWS), 1)
    diag = (lane & (SUBLANES - 1)) == (sub & (SUBLANES - 1))
    nt = (((1,), (1,)), ((), ()))

    def token(t, carry):
        w = _pair_tiles(tab_vmem, rows_ref, t)
        hrow = h_ref[pl.ds(t, 1), :]
        xt = jnp.concatenate([hrow[:, r * LANES:(r + 1) * LANES] for r in range(SUBLANES)], axis=0).astype(BF16)
        z = jnp.zeros_like(xt)
        x2 = jnp.concatenate([jnp.concatenate([xt, z], axis=1),
                              jnp.concatenate([z, xt], axis=1)], axis=0)
        res = lax.dot_general(x2, w, nt, preferred_element_type=F32)
        m = jnp.where(diag, res, 0.0)
        stage[pl.ds(t, 1), 0:W_ROWS] = jnp.sum(m[:SUBLANES], axis=0, keepdims=True)
        stage[pl.ds(t, 1), W_ROWS:] = jnp.sum(m[SUBLANES:], axis=0, keepdims=True)
        return carry

    _token_loop(tb, token)
    s = stage[...]
    hi = s.astype(BF16)
    lo = (s - hi.astype(F32)).astype(BF16)
    a = (jnp.dot(hi, gsum_ref[...], preferred_element_type=F32)
         + jnp.dot(lo, gsum_ref[...], preferred_element_type=F32))
    coef = (g_ref[...] * jax.nn.gelu(a)).astype(BF16)
    cexp_ref[...] = jnp.dot(coef, spread_ref[...], preferred_element_type=F32)


def _peer_u(rows_tm, h2, g_tm, gsum, spread, tab, *, tb):
    T = h2.shape[0]
    kern = functools.partial(_peer_u_kernel, tb=tb)
    full = lambda a: pl.BlockSpec(a.shape, lambda i: (0,) * a.ndim)
    return pl.pallas_call(
        kern,
        grid=(T // tb,),
        in_specs=[
            pl.BlockSpec((tb * N_PAIRS,), lambda i: (i,), memory_space=pltpu.SMEM),
            pl.BlockSpec((tb, D_MODEL), lambda i: (i, 0)),
            pl.BlockSpec((tb, N_PAIRS), lambda i: (i, 0)),
            full(gsum), full(spread),
            pl.BlockSpec(memory_space=pl.ANY),
        ],
        out_specs=pl.BlockSpec((tb, 2 * W_ROWS), lambda i: (i, 0)),
        out_shape=jax.ShapeDtypeStruct((T, 2 * W_ROWS), F32),
        scratch_shapes=[
            pltpu.VMEM((N_EXPERTS * HALF_TILE, LANES), jnp.uint32),
            pltpu.VMEM((tb, 2 * W_ROWS), F32),
            pltpu.SemaphoreType.DMA,
        ],
        compiler_params=_cparams(("arbitrary",), TABLE_VMEM_LIMIT),
        name="peer_u",
    )(rows_tm, h2, g_tm, gsum, spread, tab)


def _peer_v_kernel(rows_ref, cexp_ref, x_ref, g2_ref, tab_hbm, xo_ref, tab_vmem, sem, *, tb):
    _load_table(tab_hbm, tab_vmem, sem)
    g2 = g2_ref[0]
    sub = lax.broadcasted_iota(jnp.int32, (SUBLANES, LANES), 0)
    lane = lax.broadcasted_iota(jnp.int32, (SUBLANES, LANES), 1)
    diag = (lane & (SUBLANES - 1)) == sub
    n_chunks = W_ROWS // LANES

    def token(t, carry):
        w = _pair_tiles(tab_vmem, rows_ref, t)
        crow = cexp_ref[pl.ds(t, 1), :]
        blocks = []
        for blk in range(2):
            lo = [(blk * n_chunks + q) * LANES for q in range(n_chunks)]
            chunks = [jnp.where(diag, jnp.broadcast_to(crow[:, c:c + LANES], diag.shape), 0.0) for c in lo]
            blocks.append(jnp.concatenate(chunks, axis=1))
        c = jnp.concatenate(blocks, axis=0).astype(BF16)
        res = jnp.dot(c, w, preferred_element_type=F32)
        out = g2 * (res[:SUBLANES, :LANES] + res[SUBLANES:, LANES:])
        delta = jnp.concatenate([out[r:r + 1, :] for r in range(SUBLANES)], axis=1)
        xo_ref[pl.ds(t, 1), :] = x_ref[pl.ds(t, 1), :] + delta
        return carry

    _token_loop(tb, token)


def _peer_v(rows_tm, cexp, x, g2, tab, *, seq, tb):
    T = x.shape[0]
    kern = functools.partial(_peer_v_kernel, tb=tb)
    tok = pl.BlockSpec((tb, D_MODEL), lambda i: (i, 0))
    return pl.pallas_call(
        kern,
        grid=(T // tb,),
        in_specs=[
            pl.BlockSpec((tb * N_PAIRS,), lambda i: (i,), memory_space=pltpu.SMEM),
            pl.BlockSpec((tb, 2 * W_ROWS), lambda i: (i, 0)),
            tok,
            pl.BlockSpec((1, SUBLANES, LANES), lambda i: ((i * tb) // seq, 0, 0)),
            pl.BlockSpec(memory_space=pl.ANY),
        ],
        out_specs=tok,
        out_shape=jax.ShapeDtypeStruct(x.shape, F32),
        scratch_shapes=[
            pltpu.VMEM((N_EXPERTS * HALF_TILE, LANES), jnp.uint32),
            pltpu.SemaphoreType.DMA,
        ],
        compiler_params=_cparams(("arbitrary",), TABLE_VMEM_LIMIT),
        name="peer_v",
    )(rows_tm, cexp, x, g2, tab)


def _final_kernel(x_ref, g_ref, o_ref):
    o_ref[...] = _rms(x_ref[...]) * g_ref[...]


def _final_norm(x, gain, tb):
    T, D = x.shape
    return pl.pallas_call(
        _final_kernel,
        grid=(T // tb,),
        in_specs=[pl.BlockSpec((tb, D), lambda i: (i, 0)), pl.BlockSpec((1, D), lambda i: (0, 0))],
        out_specs=pl.BlockSpec((tb, D), lambda i: (i, 0)),
        out_shape=jax.ShapeDtypeStruct((T, D), F32),
        compiler_params=_cparams(("arbitrary",)),
        name="final_norm",
    )(x, gain)


PACK_ROWS = 512


def _pack_kernel(x_ref, o_ref):
    bits = pltpu.bitcast(x_ref[...].astype(BF16).astype(F32), jnp.uint32)
    for s in range(HALF_TILE):
        lo = bits[:, (2 * s) * LANES:(2 * s + 1) * LANES] >> 16
        hi = bits[:, (2 * s + 1) * LANES:(2 * s + 2) * LANES] & jnp.uint32(0xFFFF0000)
        o_ref[pl.ds(s, PACK_ROWS, stride=HALF_TILE), :] = hi | lo


def _pack_table(tab):
    lead = tab.shape[:-2]
    flat = tab.reshape(-1, D_MODEL)
    n = flat.shape[0]
    packed = pl.pallas_call(
        _pack_kernel,
        grid=(n // PACK_ROWS,),
        in_specs=[pl.BlockSpec((PACK_ROWS, D_MODEL), lambda i: (i, 0))],
        out_specs=pl.BlockSpec((PACK_ROWS * HALF_TILE, LANES), lambda i: (i, 0)),
        out_shape=jax.ShapeDtypeStruct((n * HALF_TILE, LANES), jnp.uint32),
        compiler_params=_cparams(("arbitrary",)),
        name="pack_table",
    )(flat)
    return packed.reshape(*lead, N_EXPERTS * HALF_TILE, LANES)


def _pad_heads(w, per_head, offset):
    k = w.shape[0]
    w = w.reshape(k, MLA_HEADS, per_head)
    out = jnp.zeros((k, MLA_HEADS, HEAD_PAD), w.dtype)
    out = out.at[:, :, offset:offset + per_head].set(w)
    return out.reshape(k, MLA_HEADS * HEAD_PAD)


def _block_size(seq, want):
    tb = min(want, seq)
    assert seq % tb == 0
    return tb


def _layer(xf, mod_l, cos, sin, w_in, w_s, b_s, q_norm, w_uq, kv_norm, w_ukv, w_pool, pool_scale,
           w_a, w_b, w_c, w_o, w_pq, sub_keys, u_packed, v_packed, *, batch, seq):
    T, D = xf.shape
    tb_in = _block_size(seq, 1024)
    tb_mg = _block_size(seq, 512)
    tb_rt = _block_size(seq, 1024)
    ta = _block_size(seq, 2048)
    tb_peer = LANES

    sp = (0, 256, 512, 896, 1152, 1184, 1440)
    kr_pad = jnp.zeros((D, LANES), F32).at[:, QK_NOPE:QK_NOPE + QK_ROPE].set(w_in[:, sp[4]:sp[5]])
    w1 = jnp.concatenate([w_in[:, :sp[4]], w_in[:, sp[5]:sp[6]], kr_pad], axis=1).astype(BF16)
    wg = w_in[:, sp[6]:].astype(BF16)
    bs_full = jnp.repeat(b_s.T, A_GROUP_DIM, axis=1)
    wuq = _pad_heads(w_uq, QK_DIM, 0).astype(BF16)
    wukv = w_ukv.reshape(KV_LORA, MLA_HEADS, QK_NOPE + V_DIM)
    wk = _pad_heads(wukv[:, :, :QK_NOPE].reshape(KV_LORA, -1), QK_NOPE, 0).astype(BF16)
    wv = _pad_heads(wukv[:, :, QK_NOPE:].reshape(KV_LORA, -1), V_DIM, 0).astype(BF16)
    wpool = jax.scipy.linalg.block_diag(*[w_pool[g] for g in range(len(POOL_WINDOWS))]).astype(BF16)
    wb = w_b.reshape(MLA_HEADS, V_DIM, D)
    wb = jnp.pad(wb, ((0, 0), (0, HEAD_PAD - V_DIM), (0, 0))).reshape(MLA_HEADS * HEAD_PAD, D).astype(BF16)

    ya, yc, q, k, v = _inproj(
        xf, mod_l, w1, w_s, bs_full, q_norm.reshape(1, -1), wuq, kv_norm.reshape(1, -1),
        wk, wv, cos, sin, wpool, pool_scale.reshape(1, -1), seq=seq, tb=tb_in)
    yb = _attention(q, k.T, v, batch=batch, seq=seq, ta=ta)
    x1 = _merge(xf, mod_l, ya, yb, yc, wg, w_a.astype(BF16), wb, w_c.astype(BF16),
                w_o.astype(BF16), seq=seq, tb=tb_mg)
    h2, idx_t, g_t = _route(x1, mod_l, w_pq.astype(BF16), sub_keys.astype(BF16), seq=seq, tb=tb_rt)
    rows_tm = (idx_t.T * HALF_TILE).reshape(-1)
    gsum, spread = (jnp.asarray(a, BF16) for a in _expansion_constants())
    cexp = _peer_u(rows_tm, h2, g_t.T, gsum, spread,
                   u_packed, tb=tb_peer)
    g2 = mod_l[:, 5].reshape(batch, SUBLANES, LANES)
    x2 = _peer_v(rows_tm, cexp, x1, g2, v_packed, seq=seq, tb=tb_peer)
    return x2, (x1, h2)


def kernel(x, c, positions, w_mod, b_mod, w_in, w_s, b_s, q_norm, w_uq, kv_norm, w_ukv, w_pool, pool_scale, w_a, w_b, w_c, w_o, w_pq, sub_keys, u_tab, v_tab, final_norm):
    B, S, D = x.shape
    L = w_mod.shape[0]
    T = B * S
    assert D == D_MODEL and S % A_CHUNK == 0

    c_pad = jnp.zeros((SUBLANES, D), F32).at[:B].set(c)
    mod = _modulation(c_pad, w_mod, b_mod)[:, :B].reshape(L, B, 6, D)
    mod = jnp.pad(mod, ((0, 0), (0, 0), (0, SUBLANES - 6), (0, 0)))

    inv_freq = 1.0 / (ROPE_THETA ** (jnp.arange(0, QK_ROPE, 2, dtype=F32) / QK_ROPE))
    half = QK_ROPE // 2
    freq_lane = jnp.zeros((LANES,), F32)
    freq_lane = freq_lane.at[QK_NOPE:QK_NOPE + half].set(inv_freq)
    freq_lane = freq_lane.at[QK_NOPE + half:QK_NOPE + QK_ROPE].set(inv_freq)
    ang = positions.astype(F32).reshape(T, 1) * freq_lane[None, :]
    cos, sin = _rope_tables(ang, _block_size(T, 1024))

    u_packed = _pack_table(u_tab)
    v_packed = _pack_table(v_tab)
    xf = x.reshape(T, D)
    for l in range(L):
        xf, _ = _layer(xf, mod[l], cos, sin, w_in[l], w_s[l], b_s[l], q_norm[l], w_uq[l], kv_norm[l],
                       w_ukv[l], w_pool[l], pool_scale[l], w_a[l], w_b[l], w_c[l], w_o[l], w_pq[l],
                       sub_keys[l], u_packed[l], v_packed[l], batch=B, seq=S)

    out = _final_norm(xf, final_norm.reshape(1, D), _block_size(T, 1024))
    return out.reshape(B, S, D)
```
